```python
import math
import jax, jax.numpy as jnp
from jax import lax
import numpy as np

D_MODEL = 2048
BATCH = 16
SEQ = 2048
DEPTH = 1

CTX_LEN = 256
GRID_W = 64
D_MIX = D_MODEL
ATT_WIDTH = D_MIX // 2
ATT_HEADS = 8
ATT_VDIM = ATT_WIDTH // ATT_HEADS
ATT_QKDIM = ATT_VDIM // 2
MLP_WIDTH = D_MIX - ATT_WIDTH
MLP_GROUPS = 8
MLP_GDIM = MLP_WIDTH // MLP_GROUPS
CHUNK = 128
Q_BLOCK = 128
ROPE_THETA = 10000.0
EPS = 1e-6
D_IN = 4 * ATT_WIDTH + 3 * MLP_WIDTH
SPLITS = (ATT_WIDTH, 2 * ATT_WIDTH, 3 * ATT_WIDTH, 4 * ATT_WIDTH,
          4 * ATT_WIDTH + MLP_WIDTH, 4 * ATT_WIDTH + 2 * MLP_WIDTH)

kernel_name = "hymba_diffattn_chunkmlp_dit_layer"


def rmsnorm(x, g):
    xf = x.astype(jnp.float32)
    y = xf * lax.rsqrt(jnp.mean(xf * xf, axis=-1, keepdims=True) + EPS)
    return (y * g.astype(jnp.float32)).astype(x.dtype)


def layernorm(x, g, b):
    xf = x.astype(jnp.float32)
    mu = jnp.mean(xf, axis=-1, keepdims=True)
    xc = xf - mu
    var = jnp.mean(xc * xc, axis=-1, keepdims=True)
    y = xc * lax.rsqrt(var + EPS) * g.astype(jnp.float32) + b.astype(jnp.float32)
    return y.astype(x.dtype)


def lambda_init(layer_idx):
    return 0.8 - 0.6 * math.exp(-0.3 * layer_idx)


def modulation(cond, w_ada, b_ada):
    m = jax.nn.silu(cond) @ w_ada + b_ada
    return jnp.split(m, 3, axis=-1)


def axial_rope_tables(rows):
    n_freq = ATT_QKDIM // 4
    inv = ROPE_THETA ** (-jnp.arange(n_freq, dtype=jnp.float32) / n_freq)
    row = jnp.repeat(jnp.arange(rows, dtype=jnp.float32), GRID_W)
    col = jnp.tile(jnp.arange(GRID_W, dtype=jnp.float32), rows)
    ang = jnp.stack([row[:, None] * inv, col[:, None] * inv], axis=0)
    return jnp.cos(ang), jnp.sin(ang)


def apply_axial_rope(x, cos, sin):
    xf = x.astype(jnp.float32)
    half = ATT_QKDIM // 2
    quarter = half // 2
    parts = []
    for axis in range(2):
        xa = xf[..., axis * half:(axis + 1) * half]
        x1, x2 = xa[..., :quarter], xa[..., quarter:]
        cs = cos[axis][:, None, None, :]
        sn = sin[axis][:, None, None, :]
        parts += [x1 * cs - x2 * sn, x2 * cs + x1 * sn]
    return jnp.concatenate(parts, axis=-1).astype(x.dtype)


def diff_attend(q, k, v, lam):
    s = jnp.einsum('bqhmd,bkhmd->bhmqk', q, k).astype(jnp.float32) * (ATT_QKDIM ** -0.5)
    p = jax.nn.softmax(s, axis=-1)
    p = p[:, :, 0] - lam * p[:, :, 1]
    return jnp.einsum('bhqk,bkhd->bqhd', p.astype(v.dtype), v)


def diff_attend_blocked(q, k, v, lam):
    b, l = q.shape[0], q.shape[1]
    nb = l // Q_BLOCK
    qb = jnp.moveaxis(q.reshape(b, nb, Q_BLOCK, ATT_HEADS, 2, ATT_QKDIM), 1, 0)
    ob = lax.map(lambda qi: diff_attend(qi, k, v, lam), qb)
    return jnp.moveaxis(ob, 0, 1).reshape(b, l, ATT_HEADS, ATT_VDIM)


def chunk_spatial_gating(u, v_b, ln_g, ln_b, w_s, b_s):
    b, l, _ = u.shape
    vn = layernorm(v_b, ln_g, ln_b).reshape(b, l // CHUNK, CHUNK, MLP_GROUPS, MLP_GDIM)
    z = jnp.einsum('gpq,bnqgc->bnpgc', w_s, vn) + b_s.T[:, :, None]
    return u * z.reshape(b, l, MLP_WIDTH)


def merge_project(a, g_a, m, g_b, w_out):
    y = jnp.concatenate([a * jax.nn.silu(g_a), m * jax.nn.silu(g_b)], axis=-1)
    return y @ w_out


def setup_inputs(seed: int = 0) -> dict:
    key = jax.random.key(seed)
    ks = jax.random.split(key, 20)
    f = jnp.float32
    nrm = lambda k, shape, s: jax.random.normal(k, shape, f) * s
    return {
        "x": nrm(ks[0], (BATCH, SEQ, D_MODEL), 1.0),
        "c": nrm(ks[1], (BATCH, D_MODEL), 1.0),
        "ctx": nrm(ks[2], (BATCH, CTX_LEN, D_MODEL), 1.0),
        "c_ctx": nrm(ks[3], (D_MODEL,), 1.0),
        "w_ada": nrm(ks[4], (DEPTH, D_MODEL, 3 * D_MODEL), D_MODEL ** -0.5),
        "b_ada": nrm(ks[5], (DEPTH, 3 * D_MODEL), 0.01),
        "norm_g": 1.0 + nrm(ks[6], (DEPTH, D_MODEL), 0.02),
        "w_in": nrm(ks[7], (DEPTH, D_MODEL, D_IN), D_MODEL ** -0.5),
        "w_out": nrm(ks[8], (DEPTH, D_MIX, D_MODEL), D_MIX ** -0.5),
        "lam_q1": nrm(ks[9], (DEPTH, ATT_QKDIM), 0.1),
        "lam_k1": nrm(ks[10], (DEPTH, ATT_QKDIM), 0.1),
        "lam_q2": nrm(ks[11], (DEPTH, ATT_QKDIM), 0.1),
        "lam_k2": nrm(ks[12], (DEPTH, ATT_QKDIM), 0.1),
        "subln_g": 1.0 + nrm(ks[13], (DEPTH, ATT_VDIM), 0.02),
        "mlp_ln_g": 1.0 + nrm(ks[14], (DEPTH, MLP_WIDTH), 0.02),
        "mlp_ln_b": nrm(ks[15], (DEPTH, MLP_WIDTH), 0.01),
        "w_s": nrm(ks[16], (DEPTH, MLP_GROUPS, CHUNK, CHUNK), CHUNK ** -0.5),
        "b_s": 1.0 + nrm(ks[17], (DEPTH, MLP_GROUPS, CHUNK), 0.02),
        "final_g": 1.0 + nrm(ks[18], (D_MODEL,), 0.02),
    }


def reference(x, c, ctx, c_ctx, w_ada, b_ada, norm_g, w_in, w_out, lam_q1, lam_k1, lam_q2, lam_k2,
              subln_g, mlp_ln_g, mlp_ln_b, w_s, b_s, final_g):
    b, n_lat = x.shape[0], x.shape[1]
    n_ctx = ctx.shape[1]
    rows = n_lat // GRID_W
    cos, sin = axial_rope_tables(rows)
    for i in range(DEPTH):
        last = i == DEPTH - 1
        lam0 = lambda_init(i)
        lam = (jnp.exp(jnp.sum(lam_q1[i] * lam_k1[i]).astype(jnp.float32))
               - jnp.exp(jnp.sum(lam_q2[i] * lam_k2[i]).astype(jnp.float32)) + lam0)
        shift, scale, gate = modulation(c, w_ada[i], b_ada[i])
        shift_c, scale_c, gate_c = modulation(c_ctx, w_ada[i], b_ada[i])
        h = rmsnorm(x, norm_g[i]) * (1.0 + scale[:, None]) + shift[:, None]
        hc = rmsnorm(ctx, norm_g[i]) * (1.0 + scale_c) + shift_c

        q, k, v, g_a, u, v_b, g_b = jnp.split(h @ w_in[i], SPLITS, axis=-1)
        q = apply_axial_rope(q.reshape(b, n_lat, ATT_HEADS, 2, ATT_QKDIM), cos, sin)
        k = apply_axial_rope(k.reshape(b, n_lat, ATT_HEADS, 2, ATT_QKDIM), cos, sin)
        v = v.reshape(b, n_lat, ATT_HEADS, ATT_VDIM)

        if last:
            kc, vc = jnp.split(hc @ w_in[i][:, ATT_WIDTH:3 * ATT_WIDTH], 2, axis=-1)
        else:
            qc, kc, vc, g_ac, uc, v_bc, g_bc = jnp.split(hc @ w_in[i], SPLITS, axis=-1)
        kc = kc.reshape(b, n_ctx, ATT_HEADS, 2, ATT_QKDIM)
        vc = vc.reshape(b, n_ctx, ATT_HEADS, ATT_VDIM)

        k_all = jnp.concatenate([k, kc], axis=1)
        v_all = jnp.concatenate([v, vc], axis=1)
        a = diff_attend_blocked(q, k_all, v_all, lam)
        a = (rmsnorm(a, subln_g[i]) * (1.0 - lam0)).reshape(b, n_lat, ATT_WIDTH)
        m = chunk_spatial_gating(u, v_b, mlp_ln_g[i], mlp_ln_b[i], w_s[i], b_s[i])
        y = merge_project(a, g_a, m, g_b, w_out[i])

        if not last:
            qc = qc.reshape(b, n_ctx, ATT_HEADS, 2, ATT_QKDIM)
            ac = diff_attend(qc, kc, vc, lam)
            ac = (rmsnorm(ac, subln_g[i]) * (1.0 - lam0)).reshape(b, n_ctx, ATT_WIDTH)
            mc = chunk_spatial_gating(uc, v_bc, mlp_ln_g[i], mlp_ln_b[i], w_s[i], b_s[i])
            yc = merge_project(ac, g_ac, mc, g_bc, w_out[i])
            ctx = ctx + gate_c * yc
        x = x + gate[:, None] * y
    return rmsnorm(x, final_g)
```

```python
import functools
import math

import jax
import jax.numpy as jnp
from jax import lax
from jax.experimental import pallas as pl
from jax.experimental.pallas import tpu as pltpu

D_MODEL = 2048
GRID_W = 64
ATT_WIDTH = 1024
ATT_HEADS = 8
ATT_VDIM = 128
ATT_QKDIM = 64
MLP_WIDTH = 1024
MLP_GROUPS = 8
MLP_GDIM = 128
CHUNK = 128
ROPE_THETA = 10000.0
EPS = 1e-6
D_IN = 4 * ATT_WIDTH + 3 * MLP_WIDTH
LANES = 128
COL_BLOCK = 1024
LOG2E = 1.4426950408889634

BLK_Q, BLK_K, BLK_V, BLK_GA, BLK_U, BLK_VB, BLK_GB = range(7)

_VMEM_LIMIT = 56 * 1024 * 1024


def _silu(v):
    return v * (1.0 / (1.0 + jnp.exp(-v)))


def _modulation_kernel(cond_ref, w_ref, b_ref, lq1_ref, lk1_ref, lq2_ref, lk2_ref, mod_ref, lam_ref, *, lam0):
    s = _silu(cond_ref[...]).astype(jnp.bfloat16)
    acc = jnp.dot(s, w_ref[...].astype(jnp.bfloat16), preferred_element_type=jnp.float32)
    mod_ref[...] = acc + b_ref[...]
    d1 = jnp.sum(lq1_ref[...] * lk1_ref[...], axis=-1, keepdims=True)
    d2 = jnp.sum(lq2_ref[...] * lk2_ref[...], axis=-1, keepdims=True)
    lam = jnp.exp(d1) - jnp.exp(d2) + lam0
    lam_ref[...] = jnp.broadcast_to(lam, lam_ref.shape)


def _modulation(cond, w_ada, b_ada, lq1, lk1, lq2, lk2, lam0):
    rows, d = cond.shape
    n = w_ada.shape[1]
    tn = 768
    vec = pl.BlockSpec((1, ATT_QKDIM), lambda j: (0, 0))
    return pl.pallas_call(
        functools.partial(_modulation_kernel, lam0=lam0),
        grid=(n // tn,),
        in_specs=[pl.BlockSpec((rows, d), lambda j: (0, 0)),
                  pl.BlockSpec((d, tn), lambda j: (0, j)),
                  pl.BlockSpec((1, tn), lambda j: (0, j)),
                  vec, vec, vec, vec],
        out_specs=[pl.BlockSpec((rows, tn), lambda j: (0, j)),
                   pl.BlockSpec((8, LANES), lambda j: (0, 0))],
        out_shape=[jax.ShapeDtypeStruct((rows, n), jnp.float32),
                   jax.ShapeDtypeStruct((8, LANES), jnp.float32)],
        compiler_params=pltpu.CompilerParams(dimension_semantics=("arbitrary",),
                                             vmem_limit_bytes=_VMEM_LIMIT),
        name="modulation",
    )(cond, w_ada, b_ada, lq1, lk1, lq2, lk2)


def _rope(acc, cos, sin):
    lane = lax.broadcasted_iota(jnp.int32, (1, LANES), 1)
    first = (lane % 32) < 16
    outs = []
    for j in range(acc.shape[1] // LANES):
        xg = acc[:, j * LANES:(j + 1) * LANES]
        partner = jnp.where(first, pltpu.roll(xg, LANES - 16, axis=1), pltpu.roll(xg, 16, axis=1))
        outs.append(xg * cos + partner * sin)
    return jnp.concatenate(outs, axis=1)


def _in_proj_kernel(x_ref, shift_ref, scale_ref, g_ref, w_ref, rope_ref, lng_ref, lnb_ref, o_ref, h_ref,
                    *, blocks):
    n = pl.program_id(1)

    @pl.when(n == 0)
    def _():
        xf = x_ref[...]
        ms = jnp.mean(xf * xf, axis=-1, keepdims=True)
        y = xf * lax.rsqrt(ms + EPS) * g_ref[...]
        h = y * (1.0 + scale_ref[0]) + shift_ref[0]
        h_ref[...] = h.astype(jnp.bfloat16)

    acc = jnp.dot(h_ref[...], w_ref[...], preferred_element_type=jnp.float32)

    for idx, blk in enumerate(blocks):
        @pl.when(n == idx)
        def _(blk=blk):
            if blk == "rope_q":
                out = _rope(acc, rope_ref[0], rope_ref[1])
            elif blk == "rope_k":
                out = _rope(acc, rope_ref[2], rope_ref[3])
            elif blk == "silu":
                out = _silu(acc)
            elif blk == "layernorm":
                mu = jnp.mean(acc, axis=-1, keepdims=True)
                xc = acc - mu
                var = jnp.mean(xc * xc, axis=-1, keepdims=True)
                out = xc * lax.rsqrt(var + EPS) * lng_ref[...] + lnb_ref[...]
            else:
                out = acc
            o_ref[...] = out.astype(o_ref.dtype)


def _in_proj(x2d, mod3, mod_row, norm_g, w_bf, rope, ln_g, ln_b, *, tm, blocks, first_block, rows_per_mod):
    m, d = x2d.shape
    nblk = len(blocks)
    tiles_per_seq = rope.shape[1] // tm
    return pl.pallas_call(
        functools.partial(_in_proj_kernel, blocks=blocks),
        grid=(m // tm, nblk),
        in_specs=[pl.BlockSpec((tm, d), lambda i, n: (i, 0)),
                  pl.BlockSpec((1, 1, d), lambda i, n: (mod_row(i), 0, 0)),
                  pl.BlockSpec((1, 1, d), lambda i, n: (mod_row(i), 0, 1)),
                  pl.BlockSpec((1, d), lambda i, n: (0, 0)),
                  pl.BlockSpec((d, COL_BLOCK), lambda i, n: (0, n + first_block)),
                  pl.BlockSpec((4, tm, LANES), lambda i, n: (0, i % tiles_per_seq, 0)),
                  pl.BlockSpec((1, COL_BLOCK), lambda i, n: (0, 0)),
                  pl.BlockSpec((1, COL_BLOCK), lambda i, n: (0, 0))],
        out_specs=pl.BlockSpec((tm, COL_BLOCK), lambda i, n: (i, n)),
        out_shape=jax.ShapeDtypeStruct((m, nblk * COL_BLOCK), jnp.bfloat16),
        scratch_shapes=[pltpu.VMEM((tm, d), jnp.bfloat16)],
        compiler_params=pltpu.CompilerParams(dimension_semantics=("parallel", "arbitrary"),
                                             vmem_limit_bytes=_VMEM_LIMIT),
        name="in_proj",
    )(x2d, mod3, mod3, norm_g, w_bf, rope, ln_g, ln_b)


def _attention_kernel(q_ref, k_ref, v_ref, kc_ref, vc_ref, ga_ref, sg_ref, lam_ref, o_ref, *, post_scale):
    q = q_ref[0]
    k = k_ref[0]
    kc = kc_ref[0]
    v = v_ref[0]
    vc = vc_ref[0]
    nt = (((1,), (1,)), ((), ()))
    outs = []
    for m in range(2):
        qm = q[:, m * ATT_QKDIM:(m + 1) * ATT_QKDIM]
        s_lat = lax.dot_general(qm, k[:, m * ATT_QKDIM:(m + 1) * ATT_QKDIM], nt,
                                preferred_element_type=jnp.float32)
        s_ctx = lax.dot_general(qm, kc[:, m * ATT_QKDIM:(m + 1) * ATT_QKDIM], nt,
                                preferred_element_type=jnp.float32)
        mx = jnp.maximum(jnp.max(s_lat, axis=-1, keepdims=True), jnp.max(s_ctx, axis=-1, keepdims=True))
        e_lat = jnp.exp2(s_lat - mx)
        e_ctx = jnp.exp2(s_ctx - mx)
        l = jnp.sum(e_lat, axis=-1, keepdims=True) + jnp.sum(e_ctx, axis=-1, keepdims=True)
        o = (jnp.dot(e_lat.astype(jnp.bfloat16), v, preferred_element_type=jnp.float32)
             + jnp.dot(e_ctx.astype(jnp.bfloat16), vc, preferred_element_type=jnp.float32))
        outs.append(o * (1.0 / l))
    a = outs[0] - lam_ref[0:1, :] * outs[1]
    ms = jnp.mean(a * a, axis=-1, keepdims=True)
    a = a * lax.rsqrt(ms + EPS) * sg_ref[...] * post_scale
    o_ref[0] = (a * ga_ref[0].astype(jnp.float32)).astype(o_ref.dtype)


def _attention(proj, ctxp, subln_g, lam_tile, *, tq, post_scale):
    b, l, _ = proj.shape
    nctx = ctxp.shape[1]
    hb = ATT_WIDTH // ATT_VDIM
    return pl.pallas_call(
        functools.partial(_attention_kernel, post_scale=post_scale),
        grid=(b, ATT_HEADS, l // tq),
        in_specs=[pl.BlockSpec((1, tq, ATT_VDIM), lambda bi, h, i: (bi, i, BLK_Q * hb + h)),
                  pl.BlockSpec((1, l, ATT_VDIM), lambda bi, h, i: (bi, 0, BLK_K * hb + h)),
                  pl.BlockSpec((1, l, ATT_VDIM), lambda bi, h, i: (bi, 0, BLK_V * hb + h)),
                  pl.BlockSpec((1, nctx, ATT_VDIM), lambda bi, h, i: (bi, 0, h)),
                  pl.BlockSpec((1, nctx, ATT_VDIM), lambda bi, h, i: (bi, 0, hb + h)),
                  pl.BlockSpec((1, tq, ATT_VDIM), lambda bi, h, i: (bi, i, BLK_GA * hb + h)),
                  pl.BlockSpec((1, ATT_VDIM), lambda bi, h, i: (0, 0)),
                  pl.BlockSpec((8, LANES), lambda bi, h, i: (0, 0))],
        out_specs=pl.BlockSpec((1, tq, ATT_VDIM), lambda bi, h, i: (bi, i, h)),
        out_shape=jax.ShapeDtypeStruct((b, l, ATT_WIDTH), jnp.bfloat16),
        compiler_params=pltpu.CompilerParams(dimension_semantics=("parallel", "parallel", "arbitrary"),
                                             vmem_limit_bytes=_VMEM_LIMIT),
        name="diff_attention",
    )(proj, proj, proj, ctxp, ctxp, proj, subln_g, lam_tile)


def _chunk_mlp_kernel(u_ref, vn_ref, gb_ref, ws_ref, bs_ref, o_ref):
    rows = u_ref.shape[1]
    for c in range(rows // CHUNK):
        r = slice(c * CHUNK, (c + 1) * CHUNK)
        for g in range(MLP_GROUPS):
            cs = slice(g * MLP_GDIM, (g + 1) * MLP_GDIM)
            z = jnp.dot(ws_ref[g], vn_ref[0, r, cs], preferred_element_type=jnp.float32) + bs_ref[g]
            m = u_ref[0, r, cs].astype(jnp.float32) * z * gb_ref[0, r, cs].astype(jnp.float32)
            o_ref[0, r, cs] = m.astype(o_ref.dtype)


def _chunk_mlp(proj, ws_bf, bs_b, *, rows):
    b, l, _ = proj.shape
    blk = lambda col: pl.BlockSpec((1, rows, MLP_WIDTH), lambda bi, i: (bi, i, col))
    return pl.pallas_call(
        _chunk_mlp_kernel,
        grid=(b, l // rows),
        in_specs=[blk(BLK_U), blk(BLK_VB), blk(BLK_GB),
                  pl.BlockSpec((MLP_GROUPS, CHUNK, CHUNK), lambda bi, i: (0, 0, 0)),
                  pl.BlockSpec((MLP_GROUPS, CHUNK, LANES), lambda bi, i: (0, 0, 0))],
        out_specs=pl.BlockSpec((1, rows, MLP_WIDTH), lambda bi, i: (bi, i, 0)),
        out_shape=jax.ShapeDtypeStruct((b, l, MLP_WIDTH), jnp.bfloat16),
        compiler_params=pltpu.CompilerParams(dimension_semantics=("parallel", "arbitrary"),
                                             vmem_limit_bytes=_VMEM_LIMIT),
        name="chunk_mlp",
    )(proj, proj, proj, ws_bf, bs_b)


def _out_proj_kernel(x_ref, ya_ref, yb_ref, wa_ref, wb_ref, gate_ref, fg_ref, o_ref):
    y = (jnp.dot(ya_ref[...], wa_ref[...], preferred_element_type=jnp.float32)
         + jnp.dot(yb_ref[...], wb_ref[...], preferred_element_type=jnp.float32))
    r = x_ref[...] + gate_ref[0] * y
    ms = jnp.mean(r * r, axis=-1, keepdims=True)
    o_ref[...] = r * lax.rsqrt(ms + EPS) * fg_ref[...]


def _out_proj(x2d, ya, yb, w_bf, mod3, final_g, *, tm, seq):
    m, d = x2d.shape
    tiles_per_seq = seq // tm
    return pl.pallas_call(
        _out_proj_kernel,
        grid=(m // tm,),
        in_specs=[pl.BlockSpec((tm, d), lambda i: (i, 0)),
                  pl.BlockSpec((tm, ATT_WIDTH), lambda i: (i, 0)),
                  pl.BlockSpec((tm, MLP_WIDTH), lambda i: (i, 0)),
                  pl.BlockSpec((ATT_WIDTH, d), lambda i: (0, 0)),
                  pl.BlockSpec((MLP_WIDTH, d), lambda i: (1, 0)),
                  pl.BlockSpec((1, 1, d), lambda i: (i // tiles_per_seq, 0, 2)),
                  pl.BlockSpec((1, d), lambda i: (0, 0))],
        out_specs=pl.BlockSpec((tm, d), lambda i: (i, 0)),
        out_shape=jax.ShapeDtypeStruct((m, d), jnp.float32),
        compiler_params=pltpu.CompilerParams(dimension_semantics=("parallel",),
                                             vmem_limit_bytes=_VMEM_LIMIT),
        name="out_proj",
    )(x2d, ya, yb, w_bf, w_bf, mod3, final_g)


def _rope_tables(seq, q_scale):
    n_freq = ATT_QKDIM // 4
    inv = ROPE_THETA ** (-jnp.arange(n_freq, dtype=jnp.float32) / n_freq)
    t = jnp.arange(seq)
    row = (t // GRID_W).astype(jnp.float32)
    col = (t % GRID_W).astype(jnp.float32)
    ang_r = row[:, None] * inv
    ang_c = col[:, None] * inv
    cos64 = jnp.concatenate([jnp.cos(ang_r), jnp.cos(ang_r), jnp.cos(ang_c), jnp.cos(ang_c)], axis=-1)
    sin64 = jnp.concatenate([-jnp.sin(ang_r), jnp.sin(ang_r), -jnp.sin(ang_c), jnp.sin(ang_c)], axis=-1)
    cos = jnp.tile(cos64, (1, LANES // ATT_QKDIM))
    sin = jnp.tile(sin64, (1, LANES // ATT_QKDIM))
    return jnp.stack([cos * q_scale, sin * q_scale, cos, sin], axis=0)


def kernel(x, c, ctx, c_ctx, w_ada, b_ada, norm_g, w_in, w_out, lam_q1, lam_k1, lam_q2, lam_k2, subln_g,
           mlp_ln_g, mlp_ln_b, w_s, b_s, final_g):
    b, seq, d = x.shape
    nctx = ctx.shape[1]
    depth = w_ada.shape[0]
    assert depth == 1 and d == D_MODEL and seq % GRID_W == 0
    i = 0
    lam0 = 0.8 - 0.6 * math.exp(-0.3 * i)

    mod_rows = -(-(b + 1) // 8) * 8
    cond = jnp.concatenate([c, c_ctx[None, :], jnp.zeros((mod_rows - b - 1, d), c.dtype)], axis=0)
    mod, lam_tile = _modulation(cond, w_ada[i], b_ada[i][None, :], lam_q1[i][None, :], lam_k1[i][None, :],
                                lam_q2[i][None, :], lam_k2[i][None, :], lam0)
    mod3 = mod.reshape(mod_rows, 1, 3 * d)

    w_in_bf = w_in[i].astype(jnp.bfloat16)
    w_out_bf = w_out[i].astype(jnp.bfloat16)
    ws_bf = w_s[i].astype(jnp.bfloat16)
    bs_b = jnp.broadcast_to(b_s[i][:, :, None], (MLP_GROUPS, CHUNK, LANES))
    rope = _rope_tables(seq, ATT_QKDIM ** -0.5 * LOG2E)
    ng = norm_g[i][None, :]
    ln_g = mlp_ln_g[i][None, :]
    ln_b = mlp_ln_b[i][None, :]

    tm = 1024
    x2d = x.reshape(b * seq, d)
    proj = _in_proj(x2d, mod3, lambda t: t // (seq // tm), ng, w_in_bf, rope, ln_g, ln_b, tm=tm,
                    blocks=("rope_q", "rope_k", "plain", "silu", "plain", "layernorm", "silu"),
                    first_block=0, rows_per_mod=seq)
    ctx2d = ctx.reshape(b * nctx, d)
    ctxp = _in_proj(ctx2d, mod3, lambda t: b, ng, w_in_bf, rope, ln_g, ln_b, tm=tm,
                    blocks=("plain", "plain"), first_block=BLK_K, rows_per_mod=nctx)

    proj3 = proj.reshape(b, seq, D_IN)
    ctxp3 = ctxp.reshape(b, nctx, 2 * ATT_WIDTH)
    ya = _attention(proj3, ctxp3, subln_g[i][None, :], lam_tile, tq=256, post_scale=1.0 - lam0)
    yb = _chunk_mlp(proj3, ws_bf, bs_b, rows=512)
    out = _out_proj(x2d, ya.reshape(b * seq, ATT_WIDTH), yb.reshape(b * seq, MLP_WIDTH), w_out_bf, mod3,
                    final_g[None, :], tm=512, seq=seq)
    return out.reshape(b, seq, d)
```

```python
import functools
import math

import jax
import jax.numpy as jnp
import numpy as np
from jax import lax
from jax.experimental import pallas as pl
from jax.experimental.pallas import tpu as pltpu

D_MODEL = 2048
GRID_W = 64
ATT_WIDTH = 1024
ATT_HEADS = 8
ATT_VDIM = 128
ATT_QKDIM = 64
MLP_WIDTH = 1024
MLP_GROUPS = 8
MLP_GDIM = 128
CHUNK = 128
ROPE_THETA = 10000.0
EPS = 1e-6
D_IN = 4 * ATT_WIDTH + 3 * MLP_WIDTH
LANES = 128
COL_BLOCK = 1024
PAIR = 2 * ATT_VDIM
N_PAIRS = ATT_HEADS // 2
MAP_LANES = ATT_QKDIM // 2
LOG2E = 1.4426950408889634

BLK_Q, BLK_K, BLK_V, BLK_GA, BLK_U, BLK_VB, BLK_GB = range(7)

_VMEM_LIMIT = 56 * 1024 * 1024


def _silu(v):
    return v * (1.0 / (1.0 + jnp.exp(-v)))


def _modulation_kernel(cond_ref, w_ref, b_ref, lq1_ref, lk1_ref, lq2_ref, lk2_ref, mod_ref, lam_ref, *, lam0):
    s = _silu(cond_ref[...]).astype(jnp.bfloat16)
    acc = jnp.dot(s, w_ref[...].astype(jnp.bfloat16), preferred_element_type=jnp.float32)
    mod_ref[...] = acc + b_ref[...]
    d1 = jnp.sum(lq1_ref[...] * lk1_ref[...], axis=-1, keepdims=True)
    d2 = jnp.sum(lq2_ref[...] * lk2_ref[...], axis=-1, keepdims=True)
    lam = jnp.exp(d1) - jnp.exp(d2) + lam0
    lam_ref[...] = jnp.broadcast_to(lam, lam_ref.shape)


def _modulation(cond, w_ada, b_ada, lq1, lk1, lq2, lk2, lam0):
    rows, d = cond.shape
    n = w_ada.shape[1]
    tn = 768
    vec = pl.BlockSpec((1, ATT_QKDIM), lambda j: (0, 0))
    return pl.pallas_call(
        functools.partial(_modulation_kernel, lam0=lam0),
        grid=(n // tn,),
        in_specs=[pl.BlockSpec((rows, d), lambda j: (0, 0)),
                  pl.BlockSpec((d, tn), lambda j: (0, j)),
                  pl.BlockSpec((1, tn), lambda j: (0, j)),
                  vec, vec, vec, vec],
        out_specs=[pl.BlockSpec((rows, tn), lambda j: (0, j)),
                   pl.BlockSpec((8, LANES), lambda j: (0, 0))],
        out_shape=[jax.ShapeDtypeStruct((rows, n), jnp.float32),
                   jax.ShapeDtypeStruct((8, LANES), jnp.float32)],
        compiler_params=pltpu.CompilerParams(dimension_semantics=("arbitrary",),
                                             vmem_limit_bytes=_VMEM_LIMIT),
        name="modulation",
    )(cond, w_ada, b_ada, lq1, lk1, lq2, lk2)


def _rope(acc, cos, sin):
    outs = []
    for p in range(acc.shape[1] // PAIR):
        x1 = acc[:, p * PAIR:p * PAIR + LANES]
        x2 = acc[:, p * PAIR + LANES:(p + 1) * PAIR]
        outs += [x1 * cos - x2 * sin, x2 * cos + x1 * sin]
    return jnp.concatenate(outs, axis=1)


def _in_proj_kernel(x_ref, shift_ref, scale_ref, g_ref, w_ref, rope_ref, lng_ref, lnb_ref, o_ref, h_ref,
                    *, blocks, row_chunk):
    n = pl.program_id(1)
    tm = x_ref.shape[0]

    def epilogue(blk, acc, rows):
        if blk == "rope_q":
            return _rope(acc, rope_ref[0, rows, :], rope_ref[1, rows, :])
        if blk == "rope_k":
            return _rope(acc, rope_ref[2, rows, :], rope_ref[3, rows, :])
        if blk == "silu":
            return _silu(acc)
        if blk == "layernorm":
            mu = jnp.mean(acc, axis=-1, keepdims=True)
            xc = acc - mu
            var = jnp.mean(xc * xc, axis=-1, keepdims=True)
            return xc * lax.rsqrt(var + EPS) * lng_ref[...] + lnb_ref[...]
        return acc

    for idx, blk in enumerate(blocks):
        @pl.when(n == idx)
        def _(idx=idx, blk=blk):
            for r in range(tm // row_chunk):
                rows = slice(r * row_chunk, (r + 1) * row_chunk)
                if idx == 0:
                    xf = x_ref[rows, :]
                    ms = jnp.mean(xf * xf, axis=-1, keepdims=True)
                    y = xf * lax.rsqrt(ms + EPS) * g_ref[...]
                    hb = (y * (1.0 + scale_ref[0]) + shift_ref[0]).astype(jnp.bfloat16)
                    h_ref[rows, :] = hb
                else:
                    hb = h_ref[rows, :]
                acc = jnp.dot(hb, w_ref[...], preferred_element_type=jnp.float32)
                o_ref[rows, :] = epilogue(blk, acc, rows).astype(o_ref.dtype)


def _in_proj(x2d, mod3, mod_row, norm_g, w_bf, rope, ln_g, ln_b, *, tm, row_chunk, blocks, first_block):
    m, d = x2d.shape
    nblk = len(blocks)
    tiles_per_seq = rope.shape[1] // tm
    return pl.pallas_call(
        functools.partial(_in_proj_kernel, blocks=blocks, row_chunk=row_chunk),
        grid=(m // tm, nblk),
        in_specs=[pl.BlockSpec((tm, d), lambda i, n: (i, 0)),
                  pl.BlockSpec((1, 1, d), lambda i, n: (mod_row(i), 0, 0)),
                  pl.BlockSpec((1, 1, d), lambda i, n: (mod_row(i), 0, 1)),
                  pl.BlockSpec((1, d), lambda i, n: (0, 0)),
                  pl.BlockSpec((d, COL_BLOCK), lambda i, n: (0, n + first_block)),
                  pl.BlockSpec((4, tm, LANES), lambda i, n: (0, i % tiles_per_seq, 0)),
                  pl.BlockSpec((1, COL_BLOCK), lambda i, n: (0, 0)),
                  pl.BlockSpec((1, COL_BLOCK), lambda i, n: (0, 0))],
        out_specs=pl.BlockSpec((tm, COL_BLOCK), lambda i, n: (i, n)),
        out_shape=jax.ShapeDtypeStruct((m, nblk * COL_BLOCK), jnp.bfloat16),
        scratch_shapes=[pltpu.VMEM((tm, d), jnp.bfloat16)],
        compiler_params=pltpu.CompilerParams(dimension_semantics=("parallel", "arbitrary"),
                                             vmem_limit_bytes=_VMEM_LIMIT),
        name="in_proj",
    )(x2d, mod3, mod3, norm_g, w_bf, rope, ln_g, ln_b)


def _attention_kernel(q_ref, k_ref, v_ref, kc_ref, vc_ref, ga_ref, sg_ref, lam_ref, o_ref,
                      kall_ref, vt_ref, s_ref, *, tq, post_scale):
    seq = k_ref.shape[1]
    n_blocks = q_ref.shape[1] // tq
    kall_ref[0:seq, :] = k_ref[0]
    kall_ref[seq:, :] = kc_ref[0]
    vt_ref[:, 0:seq] = v_ref[0].astype(jnp.float32).T.astype(jnp.bfloat16)
    vt_ref[:, seq:] = vc_ref[0].astype(jnp.float32).T.astype(jnp.bfloat16)

    chain_of_lane = (lax.broadcasted_iota(jnp.int32, (1, PAIR), 1) % LANES) // MAP_LANES
    nt = (((1,), (1,)), ((), ()))
    lam = lam_ref[0:1, 0:1]
    sg = sg_ref[...] * post_scale

    def scores(j):
        q = q_ref[0, j * tq:(j + 1) * tq, :]
        qz = jnp.concatenate([jnp.where(chain_of_lane == c, q, jnp.zeros_like(q)) for c in range(4)], axis=0)
        return lax.dot_general(kall_ref[...], qz, nt, preferred_element_type=jnp.float32)

    s_ref[0] = scores(0)
    for j in range(n_blocks):
        if j + 1 < n_blocks:
            s_ref[(j + 1) % 2] = scores(j + 1)
        s_t = s_ref[j % 2]
        mx = jnp.max(s_t, axis=0, keepdims=True)
        e_t = jnp.exp2(s_t - mx)
        inv_l = 1.0 / jnp.sum(e_t, axis=0, keepdims=True)
        e_b = e_t.astype(jnp.bfloat16)
        rows = slice(j * tq, (j + 1) * tq)
        for head in range(2):
            hc = slice(2 * head * tq, 2 * (head + 1) * tq)
            pv = jnp.dot(vt_ref[head * ATT_VDIM:(head + 1) * ATT_VDIM, :], e_b[:, hc],
                         preferred_element_type=jnp.float32) * inv_l[:, hc]
            a_t = pv[:, :tq] - lam * pv[:, tq:]
            ms = jnp.mean(a_t * a_t, axis=0, keepdims=True)
            a_t = a_t * lax.rsqrt(ms + EPS) * sg
            cols = slice(head * ATT_VDIM, (head + 1) * ATT_VDIM)
            o_ref[0, rows, cols] = (a_t.T * ga_ref[0, rows, cols].astype(jnp.float32)).astype(o_ref.dtype)


def _attention(proj, ctxp, subln_col, lam_tile, *, tq, post_scale):
    b, l, _ = proj.shape
    nctx = ctxp.shape[1]
    pb = COL_BLOCK // PAIR
    return pl.pallas_call(
        functools.partial(_attention_kernel, tq=tq, post_scale=post_scale),
        grid=(b, N_PAIRS),
        in_specs=[pl.BlockSpec((1, l, PAIR), lambda bi, p: (bi, 0, BLK_Q * pb + p)),
                  pl.BlockSpec((1, l, PAIR), lambda bi, p: (bi, 0, BLK_K * pb + p)),
                  pl.BlockSpec((1, l, PAIR), lambda bi, p: (bi, 0, BLK_V * pb + p)),
                  pl.BlockSpec((1, nctx, PAIR), lambda bi, p: (bi, 0, p)),
                  pl.BlockSpec((1, nctx, PAIR), lambda bi, p: (bi, 0, pb + p)),
                  pl.BlockSpec((1, l, PAIR), lambda bi, p: (bi, 0, BLK_GA * pb + p)),
                  pl.BlockSpec((ATT_VDIM, tq), lambda bi, p: (0, 0)),
                  pl.BlockSpec((8, LANES), lambda bi, p: (0, 0))],
        out_specs=pl.BlockSpec((1, l, PAIR), lambda bi, p: (bi, 0, p)),
        out_shape=jax.ShapeDtypeStruct((b, l, ATT_WIDTH), jnp.bfloat16),
        scratch_shapes=[pltpu.VMEM((l + nctx, PAIR), jnp.bfloat16),
                        pltpu.VMEM((PAIR, l + nctx), jnp.bfloat16),
                        pltpu.VMEM((2, l + nctx, 4 * tq), jnp.float32)],
        compiler_params=pltpu.CompilerParams(dimension_semantics=("parallel", "arbitrary"),
                                             vmem_limit_bytes=_VMEM_LIMIT),
        name="diff_attention",
    )(proj, proj, proj, ctxp, ctxp, proj, subln_col, lam_tile)


def _chunk_mlp_kernel(u_ref, vn_ref, gb_ref, ws_ref, bs_ref, o_ref):
    rows = u_ref.shape[1]
    for c in range(rows // CHUNK):
        r = slice(c * CHUNK, (c + 1) * CHUNK)
        for g in range(MLP_GROUPS):
            cs = slice(g * MLP_GDIM, (g + 1) * MLP_GDIM)
            z = jnp.dot(ws_ref[g], vn_ref[0, r, cs], preferred_element_type=jnp.float32) + bs_ref[g]
            m = u_ref[0, r, cs].astype(jnp.float32) * z * gb_ref[0, r, cs].astype(jnp.float32)
            o_ref[0, r, cs] = m.astype(o_ref.dtype)


def _chunk_mlp(proj, ws_bf, bs_b, *, rows):
    b, l, _ = proj.shape
    blk = lambda col: pl.BlockSpec((1, rows, MLP_WIDTH), lambda bi, i: (bi, i, col))
    return pl.pallas_call(
        _chunk_mlp_kernel,
        grid=(b, l // rows),
        in_specs=[blk(BLK_U), blk(BLK_VB), blk(BLK_GB),
                  pl.BlockSpec((MLP_GROUPS, CHUNK, CHUNK), lambda bi, i: (0, 0, 0)),
                  pl.BlockSpec((MLP_GROUPS, CHUNK, LANES), lambda bi, i: (0, 0, 0))],
        out_specs=pl.BlockSpec((1, rows, MLP_WIDTH), lambda bi, i: (bi, i, 0)),
        out_shape=jax.ShapeDtypeStruct((b, l, MLP_WIDTH), jnp.bfloat16),
        compiler_params=pltpu.CompilerParams(dimension_semantics=("parallel", "arbitrary"),
                                             vmem_limit_bytes=_VMEM_LIMIT),
        name="chunk_mlp",
    )(proj, proj, proj, ws_bf, bs_b)


def _out_proj_kernel(x_ref, ya_ref, yb_ref, wa_ref, wb_ref, gate_ref, fg_ref, o_ref, *, row_chunk):
    for r in range(x_ref.shape[0] // row_chunk):
        rows = slice(r * row_chunk, (r + 1) * row_chunk)
        y = (jnp.dot(ya_ref[rows, :], wa_ref[...], preferred_element_type=jnp.float32)
             + jnp.dot(yb_ref[rows, :], wb_ref[...], preferred_element_type=jnp.float32))
        res = x_ref[rows, :] + gate_ref[0] * y
        ms = jnp.mean(res * res, axis=-1, keepdims=True)
        o_ref[rows, :] = res * lax.rsqrt(ms + EPS) * fg_ref[...]


def _out_proj(x2d, ya, yb, w_bf, mod3, final_g, *, tm, row_chunk, seq):
    m, d = x2d.shape
    tiles_per_seq = seq // tm
    return pl.pallas_call(
        functools.partial(_out_proj_kernel, row_chunk=row_chunk),
        grid=(m // tm,),
        in_specs=[pl.BlockSpec((tm, d), lambda i: (i, 0)),
                  pl.BlockSpec((tm, ATT_WIDTH), lambda i: (i, 0)),
                  pl.BlockSpec((tm, MLP_WIDTH), lambda i: (i, 0)),
                  pl.BlockSpec((ATT_WIDTH, d), lambda i: (0, 0)),
                  pl.BlockSpec((MLP_WIDTH, d), lambda i: (1, 0)),
                  pl.BlockSpec((1, 1, d), lambda i: (i // tiles_per_seq, 0, 2)),
                  pl.BlockSpec((1, d), lambda i: (0, 0))],
        out_specs=pl.BlockSpec((tm, d), lambda i: (i, 0)),
        out_shape=jax.ShapeDtypeStruct((m, d), jnp.float32),
        compiler_params=pltpu.CompilerParams(dimension_semantics=("parallel",),
                                             vmem_limit_bytes=_VMEM_LIMIT),
        name="out_proj",
    )(x2d, ya, yb, w_bf, w_bf, mod3, final_g)


def _paired_column_order():
    order = np.empty(ATT_WIDTH, dtype=np.int32)
    n_freq = ATT_QKDIM // 4
    for col in range(ATT_WIDTH):
        pair, lane = divmod(col, PAIR)
        half, idx = divmod(lane, LANES)
        chain, e = divmod(idx, MAP_LANES)
        axis, f = divmod(e, n_freq)
        head = 2 * pair + chain // 2
        dim = axis * (ATT_QKDIM // 2) + half * n_freq + f
        order[col] = head * ATT_VDIM + (chain % 2) * ATT_QKDIM + dim
    return order


def _rope_tables(seq, q_scale):
    n_freq = ATT_QKDIM // 4
    inv = ROPE_THETA ** (-jnp.arange(n_freq, dtype=jnp.float32) / n_freq)
    t = jnp.arange(seq)
    row = (t // GRID_W).astype(jnp.float32)
    col = (t % GRID_W).astype(jnp.float32)
    ang = jnp.concatenate([row[:, None] * inv, col[:, None] * inv], axis=-1)
    ang = jnp.tile(ang, (1, LANES // MAP_LANES))
    cos, sin = jnp.cos(ang), jnp.sin(ang)
    return jnp.stack([cos * q_scale, sin * q_scale, cos, sin], axis=0)


def kernel(x, c, ctx, c_ctx, w_ada, b_ada, norm_g, w_in, w_out, lam_q1, lam_k1, lam_q2, lam_k2, subln_g,
           mlp_ln_g, mlp_ln_b, w_s, b_s, final_g):
    b, seq, d = x.shape
    nctx = ctx.shape[1]
    depth = w_ada.shape[0]
    assert depth == 1 and d == D_MODEL and seq % GRID_W == 0
    i = 0
    lam0 = 0.8 - 0.6 * math.exp(-0.3 * i)

    tm_in, chunk_in = 1024, 256
    tm_out, chunk_out = 512, 256
    tq = 256
    mlp_rows = 512

    mod_rows = -(-(b + 1) // 8) * 8
    cond = jnp.concatenate([c, c_ctx[None, :], jnp.zeros((mod_rows - b - 1, d), c.dtype)], axis=0)
    mod, lam_tile = _modulation(cond, w_ada[i], b_ada[i][None, :], lam_q1[i][None, :], lam_k1[i][None, :],
                                lam_q2[i][None, :], lam_k2[i][None, :], lam0)
    mod3 = mod.reshape(mod_rows, 1, 3 * d)

    order = _paired_column_order()
    col_order = np.concatenate([order, ATT_WIDTH + order, np.arange(2 * ATT_WIDTH, D_IN, dtype=np.int32)])
    w_in_bf = w_in[i][:, col_order].astype(jnp.bfloat16)
    w_out_bf = w_out[i].astype(jnp.bfloat16)
    ws_bf = w_s[i].astype(jnp.bfloat16)
    bs_b = jnp.broadcast_to(b_s[i][:, :, None], (MLP_GROUPS, CHUNK, LANES))
    subln_col = jnp.broadcast_to(subln_g[i][:, None], (ATT_VDIM, tq))
    rope = _rope_tables(seq, ATT_QKDIM ** -0.5 * LOG2E)
    ng = norm_g[i][None, :]
    ln_g = mlp_ln_g[i][None, :]
    ln_b = mlp_ln_b[i][None, :]

    x2d = x.reshape(b * seq, d)
    proj = _in_proj(x2d, mod3, lambda t: t // (seq // tm_in), ng, w_in_bf, rope, ln_g, ln_b,
                    tm=tm_in, row_chunk=chunk_in,
                    blocks=("rope_q", "rope_k", "plain", "silu", "plain", "layernorm", "silu"), first_block=0)
    ctx2d = ctx.reshape(b * nctx, d)
    ctxp = _in_proj(ctx2d, mod3, lambda t: b, ng, w_in_bf, rope, ln_g, ln_b,
                    tm=tm_in, row_chunk=chunk_in, blocks=("plain", "plain"), first_block=BLK_K)

    proj3 = proj.reshape(b, seq, D_IN)
    ctxp3 = ctxp.reshape(b, nctx, 2 * ATT_WIDTH)
    ya = _attention(proj3, ctxp3, subln_col, lam_tile, tq=tq, post_scale=1.0 - lam0)
    yb = _chunk_mlp(proj3, ws_bf, bs_b, rows=mlp_rows)
    out = _out_proj(x2d, ya.reshape(b * seq, ATT_WIDTH), yb.reshape(b * seq, MLP_WIDTH), w_out_bf, mod3,
                    final_g[None, :], tm=tm_out, row_chunk=chunk_out, seq=seq)
    return out.reshape(b, seq, d)
```

```python
import functools
import math

import jax
import jax.numpy as jnp
import numpy as np
from jax import lax
from jax.experimental import pallas as pl
from jax.experimental.pallas import tpu as pltpu

D_MODEL = 2048
GRID_W = 64
ATT_WIDTH = 1024
ATT_HEADS = 8
ATT_VDIM = 128
ATT_QKDIM = 64
MLP_WIDTH = 1024
MLP_GROUPS = 8
MLP_GDIM = 128
CHUNK = 128
ROPE_THETA = 10000.0
EPS = 1e-6
D_IN = 4 * ATT_WIDTH + 3 * MLP_WIDTH
LANES = 128
COL_BLOCK = 1024
PAIR = 2 * ATT_VDIM
N_PAIRS = ATT_HEADS // 2
MAP_LANES = ATT_QKDIM // 2
N_CHAINS = 4
S_SLOTS = 2
BF16_SUBLANES = 16
VT_ROWS = ATT_VDIM + BF16_SUBLANES
LOG2E = 1.4426950408889634

BLK_Q, BLK_K, BLK_V, BLK_GA, BLK_U, BLK_VB, BLK_GB = range(7)

_VMEM_LIMIT = 56 * 1024 * 1024


def _silu(v):
    return v * (1.0 / (1.0 + jnp.exp(-v)))


def _modulation_kernel(cond_ref, w_ref, b_ref, lq1_ref, lk1_ref, lq2_ref, lk2_ref, mod_ref, lam_ref, *, lam0):
    s = _silu(cond_ref[...]).astype(jnp.bfloat16)
    acc = jnp.dot(s, w_ref[...].astype(jnp.bfloat16), preferred_element_type=jnp.float32)
    mod_ref[...] = acc + b_ref[...]
    d1 = jnp.sum(lq1_ref[...] * lk1_ref[...], axis=-1, keepdims=True)
    d2 = jnp.sum(lq2_ref[...] * lk2_ref[...], axis=-1, keepdims=True)
    lam = jnp.exp(d1) - jnp.exp(d2) + lam0
    lam_ref[...] = jnp.broadcast_to(lam, lam_ref.shape)


def _modulation(cond, w_ada, b_ada, lq1, lk1, lq2, lk2, lam0):
    rows, d = cond.shape
    n = w_ada.shape[1]
    tn = 768
    vec = pl.BlockSpec((1, ATT_QKDIM), lambda j: (0, 0))
    return pl.pallas_call(
        functools.partial(_modulation_kernel, lam0=lam0),
        grid=(n // tn,),
        in_specs=[pl.BlockSpec((rows, d), lambda j: (0, 0)),
                  pl.BlockSpec((d, tn), lambda j: (0, j)),
                  pl.BlockSpec((1, tn), lambda j: (0, j)),
                  vec, vec, vec, vec],
        out_specs=[pl.BlockSpec((rows, tn), lambda j: (0, j)),
                   pl.BlockSpec((8, LANES), lambda j: (0, 0))],
        out_shape=[jax.ShapeDtypeStruct((rows, n), jnp.float32),
                   jax.ShapeDtypeStruct((8, LANES), jnp.float32)],
        compiler_params=pltpu.CompilerParams(dimension_semantics=("arbitrary",),
                                             vmem_limit_bytes=_VMEM_LIMIT),
        name="modulation",
    )(cond, w_ada, b_ada, lq1, lk1, lq2, lk2)


_QK_BLOCKS = ("rope_q", "rope_k", "key")


def _rope(acc, cos, sin):
    outs = []
    for p in range(acc.shape[1] // PAIR):
        x1 = acc[:, p * PAIR:p * PAIR + LANES]
        x2 = acc[:, p * PAIR + LANES:(p + 1) * PAIR]
        outs += [x1 * cos - x2 * sin, x2 * cos + x1 * sin]
    return jnp.concatenate(outs, axis=1)


def _in_proj_kernel(x_ref, shift_ref, scale_ref, g_ref, wqk_ref, wrest_ref, rope_ref, lng_ref, lnb_ref, o_ref,
                    h_ref, *, blocks, row_chunk):
    n = pl.program_id(1)
    tm = x_ref.shape[0]

    def epilogue(blk, acc, rows):
        if blk == "rope_q":
            return _rope(acc, rope_ref[0, rows, :], rope_ref[1, rows, :])
        if blk == "rope_k":
            return _rope(acc, rope_ref[2, rows, :], rope_ref[3, rows, :])
        if blk == "silu":
            return _silu(acc)
        if blk == "layernorm":
            mu = jnp.mean(acc, axis=-1, keepdims=True)
            xc = acc - mu
            var = jnp.mean(xc * xc, axis=-1, keepdims=True)
            return xc * lax.rsqrt(var + EPS) * lng_ref[...] + lnb_ref[...]
        return acc

    for idx, blk in enumerate(blocks):
        @pl.when(n == idx)
        def _(idx=idx, blk=blk):
            w_ref = wqk_ref if blk in _QK_BLOCKS else wrest_ref
            for r in range(tm // row_chunk):
                rows = slice(r * row_chunk, (r + 1) * row_chunk)
                if idx == 0:
                    xf = x_ref[rows, :]
                    ms = jnp.mean(xf * xf, axis=-1, keepdims=True)
                    y = xf * lax.rsqrt(ms + EPS) * g_ref[...]
                    hb = (y * (1.0 + scale_ref[0]) + shift_ref[0]).astype(jnp.bfloat16)
                    h_ref[rows, :] = hb
                else:
                    hb = h_ref[rows, :]
                acc = jnp.dot(hb, w_ref[...], preferred_element_type=jnp.float32)
                o_ref[rows, :] = epilogue(blk, acc, rows).astype(o_ref.dtype)


def _in_proj(x2d, mod3, mod_row, norm_g, w_qk, w_rest, rope, ln_g, ln_b, *, tm, row_chunk, blocks,
             qk_block, rest_block):
    m, d = x2d.shape
    nblk = len(blocks)
    tiles_per_seq = rope.shape[1] // tm
    return pl.pallas_call(
        functools.partial(_in_proj_kernel, blocks=blocks, row_chunk=row_chunk),
        grid=(m // tm, nblk),
        in_specs=[pl.BlockSpec((tm, d), lambda i, n: (i, 0)),
                  pl.BlockSpec((1, 1, d), lambda i, n: (mod_row(i), 0, 0)),
                  pl.BlockSpec((1, 1, d), lambda i, n: (mod_row(i), 0, 1)),
                  pl.BlockSpec((1, d), lambda i, n: (0, 0)),
                  pl.BlockSpec((d, COL_BLOCK), lambda i, n: (0, qk_block(n))),
                  pl.BlockSpec((d, COL_BLOCK), lambda i, n: (0, rest_block(n))),
                  pl.BlockSpec((4, tm, LANES), lambda i, n: (0, i % tiles_per_seq, 0)),
                  pl.BlockSpec((1, COL_BLOCK), lambda i, n: (0, 0)),
                  pl.BlockSpec((1, COL_BLOCK), lambda i, n: (0, 0))],
        out_specs=pl.BlockSpec((tm, COL_BLOCK), lambda i, n: (i, n)),
        out_shape=jax.ShapeDtypeStruct((m, nblk * COL_BLOCK), jnp.bfloat16),
        scratch_shapes=[pltpu.VMEM((tm, d), jnp.bfloat16)],
        compiler_params=pltpu.CompilerParams(dimension_semantics=("parallel", "arbitrary"),
                                             vmem_limit_bytes=_VMEM_LIMIT),
        name="in_proj",
    )(x2d, mod3, mod3, norm_g, w_qk, w_rest, rope, ln_g, ln_b)


def _attention_kernel(q_ref, k_ref, v_ref, kc_ref, vc_ref, ga_ref, sg_ref, lam_ref, o_ref,
                      kall_ref, vt_ref, s_ref, eb_ref, *, tq, key_chunk, sm_rows, post_scale):
    seq = k_ref.shape[1]
    keys = kall_ref.shape[0]
    n_blocks = q_ref.shape[1] // tq
    width = N_CHAINS * tq
    kall_ref[0:seq, :] = k_ref[0]
    kall_ref[seq:, :] = kc_ref[0]
    ones_rows = (lax.broadcasted_iota(jnp.int32, (VT_ROWS - ATT_VDIM, keys), 0) == 0).astype(jnp.bfloat16)
    for head in range(2):
        cols = slice(head * ATT_VDIM, (head + 1) * ATT_VDIM)
        base = head * VT_ROWS
        vt_ref[base:base + ATT_VDIM, 0:seq] = v_ref[0, :, cols].astype(jnp.float32).T.astype(jnp.bfloat16)
        vt_ref[base:base + ATT_VDIM, seq:] = vc_ref[0, :, cols].astype(jnp.float32).T.astype(jnp.bfloat16)
        vt_ref[base + ATT_VDIM:base + VT_ROWS, :] = ones_rows

    chain_of_row = (lax.broadcasted_iota(jnp.int32, (PAIR, 1), 0) % LANES) // MAP_LANES
    lam = lam_ref[0:1, 0:1]
    sg = sg_ref[...] * post_scale

    def block_rows(j):
        return slice(j * tq, (j + 1) * tq)

    def scores(j):
        q_t = q_ref[0, block_rows(j), :].astype(jnp.float32).T
        qz = jnp.concatenate([jnp.where(chain_of_row == c, q_t, 0.0) for c in range(N_CHAINS)],
                             axis=1).astype(jnp.bfloat16)
        s_slot = s_ref.at[j % 2]
        m8 = None
        for c in range(keys // key_chunk):
            kr = slice(c * key_chunk, (c + 1) * key_chunk)
            s_c = jnp.dot(kall_ref[kr, :], qz, preferred_element_type=jnp.float32)
            s_slot[kr, :] = s_c
            part = jnp.max(s_c.reshape(key_chunk // 8, 8, width), axis=0)
            m8 = part if m8 is None else jnp.maximum(m8, part)
        return jnp.max(m8, axis=0, keepdims=True)

    def exponentiate(j, mx):
        s_slot = s_ref.at[j % 2]
        e_slot = eb_ref.at[j % 2]
        for c in range(keys // sm_rows):
            kr = slice(c * sm_rows, (c + 1) * sm_rows)
            e_slot[kr, :] = jnp.exp2(s_slot[kr, :] - mx).astype(jnp.bfloat16)

    def finish(j):
        e_slot = eb_ref.at[j % 2]
        rows = block_rows(j)
        for head in range(2):
            hc = slice(2 * head * tq, 2 * (head + 1) * tq)
            pv = jnp.dot(vt_ref[head * VT_ROWS:(head + 1) * VT_ROWS, :], e_slot[:, hc],
                         preferred_element_type=jnp.float32)
            pv = pv[:ATT_VDIM, :] * (1.0 / pv[ATT_VDIM:ATT_VDIM + 1, :])
            a_t = pv[:, :tq] - lam * pv[:, tq:]
            ms = jnp.mean(a_t * a_t, axis=0, keepdims=True)
            a_t = a_t * lax.rsqrt(ms + EPS) * sg
            cols = slice(head * ATT_VDIM, (head + 1) * ATT_VDIM)
            o_ref[0, rows, cols] = (a_t.T * ga_ref[0, rows, cols].astype(jnp.float32)).astype(o_ref.dtype)

    mx = {}
    for t in range(n_blocks + 2):
        if t < n_blocks:
            mx[t] = scores(t)
        if 0 <= t - 1 < n_blocks:
            exponentiate(t - 1, mx.pop(t - 1))
        if 0 <= t - 2 < n_blocks:
            finish(t - 2)


def _attention(proj, ctxp, subln_col, lam_tile, *, tq, key_chunk, sm_rows, post_scale):
    b, l, _ = proj.shape
    nctx = ctxp.shape[1]
    pb = COL_BLOCK // PAIR
    return pl.pallas_call(
        functools.partial(_attention_kernel, tq=tq, key_chunk=key_chunk, sm_rows=sm_rows,
                          post_scale=post_scale),
        grid=(b, N_PAIRS),
        in_specs=[pl.BlockSpec((1, l, PAIR), lambda bi, p: (bi, 0, BLK_Q * pb + p)),
                  pl.BlockSpec((1, l, PAIR), lambda bi, p: (bi, 0, BLK_K * pb + p)),
                  pl.BlockSpec((1, l, PAIR), lambda bi, p: (bi, 0, BLK_V * pb + p)),
                  pl.BlockSpec((1, nctx, PAIR), lambda bi, p: (bi, 0, p)),
                  pl.BlockSpec((1, nctx, PAIR), lambda bi, p: (bi, 0, pb + p)),
                  pl.BlockSpec((1, l, PAIR), lambda bi, p: (bi, 0, BLK_GA * pb + p)),
                  pl.BlockSpec((ATT_VDIM, tq), lambda bi, p: (0, 0)),
                  pl.BlockSpec((8, LANES), lambda bi, p: (0, 0))],
        out_specs=pl.BlockSpec((1, l, PAIR), lambda bi, p: (bi, 0, p)),
        out_shape=jax.ShapeDtypeStruct((b, l, ATT_WIDTH), jnp.bfloat16),
        scratch_shapes=[pltpu.VMEM((l + nctx, PAIR), jnp.bfloat16),
                        pltpu.VMEM((2 * VT_ROWS, l + nctx), jnp.bfloat16),
                        pltpu.VMEM((S_SLOTS, l + nctx, N_CHAINS * tq), jnp.float32),
                        pltpu.VMEM((2, l + nctx, N_CHAINS * tq), jnp.bfloat16)],
        compiler_params=pltpu.CompilerParams(dimension_semantics=("parallel", "arbitrary"),
                                             vmem_limit_bytes=_VMEM_LIMIT),
        name="diff_attention",
    )(proj, proj, proj, ctxp, ctxp, proj, subln_col, lam_tile)


def _chunk_mlp_kernel(u_ref, vn_ref, gb_ref, ws_ref, bs_ref, o_ref):
    rows = u_ref.shape[1]
    for c in range(rows // CHUNK):
        r = slice(c * CHUNK, (c + 1) * CHUNK)
        for g in range(MLP_GROUPS):
            cs = slice(g * MLP_GDIM, (g + 1) * MLP_GDIM)
            z = jnp.dot(ws_ref[g], vn_ref[0, r, cs], preferred_element_type=jnp.float32) + bs_ref[g]
            m = u_ref[0, r, cs].astype(jnp.float32) * z * gb_ref[0, r, cs].astype(jnp.float32)
            o_ref[0, r, cs] = m.astype(o_ref.dtype)


def _chunk_mlp(proj, ws_bf, bs_b, *, rows):
    b, l, _ = proj.shape
    blk = lambda col: pl.BlockSpec((1, rows, MLP_WIDTH), lambda bi, i: (bi, i, col))
    return pl.pallas_call(
        _chunk_mlp_kernel,
        grid=(b, l // rows),
        in_specs=[blk(BLK_U), blk(BLK_VB), blk(BLK_GB),
                  pl.BlockSpec((MLP_GROUPS, CHUNK, CHUNK), lambda bi, i: (0, 0, 0)),
                  pl.BlockSpec((MLP_GROUPS, CHUNK, LANES), lambda bi, i: (0, 0, 0))],
        out_specs=pl.BlockSpec((1, rows, MLP_WIDTH), lambda bi, i: (bi, i, 0)),
        out_shape=jax.ShapeDtypeStruct((b, l, MLP_WIDTH), jnp.bfloat16),
        compiler_params=pltpu.CompilerParams(dimension_semantics=("parallel", "arbitrary"),
                                             vmem_limit_bytes=_VMEM_LIMIT),
        name="chunk_mlp",
    )(proj, proj, proj, ws_bf, bs_b)


def _out_proj_kernel(x_ref, ya_ref, yb_ref, wa_ref, wb_ref, gate_ref, fg_ref, o_ref, *, row_chunk):
    for r in range(x_ref.shape[0] // row_chunk):
        rows = slice(r * row_chunk, (r + 1) * row_chunk)
        y = (jnp.dot(ya_ref[rows, :], wa_ref[...], preferred_element_type=jnp.float32)
             + jnp.dot(yb_ref[rows, :], wb_ref[...], preferred_element_type=jnp.float32))
        res = x_ref[rows, :] + gate_ref[0] * y
        ms = jnp.mean(res * res, axis=-1, keepdims=True)
        o_ref[rows, :] = res * lax.rsqrt(ms + EPS) * fg_ref[...]


def _out_proj(x2d, ya, yb, w_bf, mod3, final_g, *, tm, row_chunk, seq):
    m, d = x2d.shape
    tiles_per_seq = seq // tm
    return pl.pallas_call(
        functools.partial(_out_proj_kernel, row_chunk=row_chunk),
        grid=(m // tm,),
        in_specs=[pl.BlockSpec((tm, d), lambda i: (i, 0)),
                  pl.BlockSpec((tm, ATT_WIDTH), lambda i: (i, 0)),
                  pl.BlockSpec((tm, MLP_WIDTH), lambda i: (i, 0)),
                  pl.BlockSpec((ATT_WIDTH, d), lambda i: (0, 0)),
                  pl.BlockSpec((MLP_WIDTH, d), lambda i: (1, 0)),
                  pl.BlockSpec((1, 1, d), lambda i: (i // tiles_per_seq, 0, 2)),
                  pl.BlockSpec((1, d), lambda i: (0, 0))],
        out_specs=pl.BlockSpec((tm, d), lambda i: (i, 0)),
        out_shape=jax.ShapeDtypeStruct((m, d), jnp.float32),
        compiler_params=pltpu.CompilerParams(dimension_semantics=("parallel",),
                                             vmem_limit_bytes=_VMEM_LIMIT),
        name="out_proj",
    )(x2d, ya, yb, w_bf, w_bf, mod3, final_g)


def _paired_layout(w_qk):
    d = w_qk.shape[0]
    n_freq = ATT_QKDIM // 4
    w = w_qk.reshape(d, 2, N_PAIRS, 2, 2, 2, 2, n_freq)
    return w.transpose(0, 1, 2, 6, 3, 4, 5, 7).reshape(d, 2 * ATT_WIDTH)


def _rope_tables(seq, q_scale):
    n_freq = ATT_QKDIM // 4
    inv = ROPE_THETA ** (-jnp.arange(n_freq, dtype=jnp.float32) / n_freq)
    t = jnp.arange(seq)
    row = (t // GRID_W).astype(jnp.float32)
    col = (t % GRID_W).astype(jnp.float32)
    ang = jnp.concatenate([row[:, None] * inv, col[:, None] * inv], axis=-1)
    ang = jnp.tile(ang, (1, LANES // MAP_LANES))
    cos, sin = jnp.cos(ang), jnp.sin(ang)
    return jnp.stack([cos * q_scale, sin * q_scale, cos, sin], axis=0)


def kernel(x, c, ctx, c_ctx, w_ada, b_ada, norm_g, w_in, w_out, lam_q1, lam_k1, lam_q2, lam_k2, subln_g,
           mlp_ln_g, mlp_ln_b, w_s, b_s, final_g):
    b, seq, d = x.shape
    nctx = ctx.shape[1]
    depth = w_ada.shape[0]
    assert depth == 1 and d == D_MODEL and seq % GRID_W == 0
    i = 0
    lam0 = 0.8 - 0.6 * math.exp(-0.3 * i)

    tm_in, chunk_in = 1024, 256
    tm_out, chunk_out = 512, 256
    tq = 256
    mlp_rows = 512

    mod_rows = -(-(b + 1) // 8) * 8
    cond = jnp.concatenate([c, c_ctx[None, :], jnp.zeros((mod_rows - b - 1, d), c.dtype)], axis=0)
    mod, lam_tile = _modulation(cond, w_ada[i], b_ada[i][None, :], lam_q1[i][None, :], lam_k1[i][None, :],
                                lam_q2[i][None, :], lam_k2[i][None, :], lam0)
    mod3 = mod.reshape(mod_rows, 1, 3 * d)

    w_qk_bf = _paired_layout(w_in[i][:, :2 * ATT_WIDTH]).astype(jnp.bfloat16)
    w_rest_bf = w_in[i][:, 2 * ATT_WIDTH:].astype(jnp.bfloat16)
    w_out_bf = w_out[i].astype(jnp.bfloat16)
    ws_bf = w_s[i].astype(jnp.bfloat16)
    bs_b = jnp.broadcast_to(b_s[i][:, :, None], (MLP_GROUPS, CHUNK, LANES))
    subln_col = jnp.broadcast_to(subln_g[i][:, None], (ATT_VDIM, tq))
    rope = _rope_tables(seq, ATT_QKDIM ** -0.5 * LOG2E)
    ng = norm_g[i][None, :]
    ln_g = mlp_ln_g[i][None, :]
    ln_b = mlp_ln_b[i][None, :]

    x2d = x.reshape(b * seq, d)
    proj = _in_proj(x2d, mod3, lambda t: t // (seq // tm_in), ng, w_qk_bf, w_rest_bf, rope, ln_g, ln_b,
                    tm=tm_in, row_chunk=chunk_in,
                    blocks=("rope_q", "rope_k", "plain", "silu", "plain", "layernorm", "silu"),
                    qk_block=lambda n: jnp.minimum(n, 1), rest_block=lambda n: jnp.maximum(n - 2, 0))
    ctx2d = ctx.reshape(b * nctx, d)
    ctxp = _in_proj(ctx2d, mod3, lambda t: b, ng, w_qk_bf, w_rest_bf, rope, ln_g, ln_b,
                    tm=tm_in, row_chunk=chunk_in, blocks=("key", "plain"),
                    qk_block=lambda n: 1, rest_block=lambda n: 0)

    proj3 = proj.reshape(b, seq, D_IN)
    ctxp3 = ctxp.reshape(b, nctx, 2 * ATT_WIDTH)
    ya = _attention(proj3, ctxp3, subln_col, lam_tile, tq=tq, key_chunk=256, sm_rows=64,
                    post_scale=1.0 - lam0)
    yb = _chunk_mlp(proj3, ws_bf, bs_b, rows=mlp_rows)
    out = _out_proj(x2d, ya.reshape(b * seq, ATT_WIDTH), yb.reshape(b * seq, MLP_WIDTH), w_out_bf, mod3,
                    final_g[None, :], tm=tm_out, row_chunk=chunk_out, seq=seq)
    return out.reshape(b, seq, d)
```

```python
import functools
import math

import jax
import jax.numpy as jnp
import numpy as np
from jax import lax
from jax.experimental import pallas as pl
from jax.experimental.pallas import tpu as pltpu

D_MODEL = 2048
GRID_W = 64
ATT_WIDTH = 1024
ATT_HEADS = 8
ATT_VDIM = 128
ATT_QKDIM = 64
MLP_WIDTH = 1024
MLP_GROUPS = 8
MLP_GDIM = 128
CHUNK = 128
ROPE_THETA = 10000.0
EPS = 1e-6
D_IN = 4 * ATT_WIDTH + 3 * MLP_WIDTH
LANES = 128
COL_BLOCK = 1024
PAIR = 2 * ATT_VDIM
N_PAIRS = ATT_HEADS // 2
MAP_LANES = ATT_QKDIM // 2
N_CHAINS = 4
S_SLOTS = 2
BF16_SUBLANES = 16
VT_ROWS = ATT_VDIM + BF16_SUBLANES
LOG2E = 1.4426950408889634

BLK_Q, BLK_K, BLK_V, BLK_GA, BLK_U, BLK_VB, BLK_GB = range(7)

_VMEM_LIMIT = 56 * 1024 * 1024


def _silu(v):
    return v * (1.0 / (1.0 + jnp.exp(-v)))


def _modulation_kernel(cond_ref, w_ref, b_ref, lq1_ref, lk1_ref, lq2_ref, lk2_ref, mod_ref, lam_ref, *, lam0):
    s = _silu(cond_ref[...]).astype(jnp.bfloat16)
    acc = jnp.dot(s, w_ref[...].astype(jnp.bfloat16), preferred_element_type=jnp.float32)
    mod_ref[...] = acc + b_ref[...]
    d1 = jnp.sum(lq1_ref[...] * lk1_ref[...], axis=-1, keepdims=True)
    d2 = jnp.sum(lq2_ref[...] * lk2_ref[...], axis=-1, keepdims=True)
    lam = jnp.exp(d1) - jnp.exp(d2) + lam0
    lam_ref[...] = jnp.broadcast_to(lam, lam_ref.shape)


def _modulation(cond, w_ada, b_ada, lq1, lk1, lq2, lk2, lam0):
    rows, d = cond.shape
    n = w_ada.shape[1]
    tn = 768
    vec = pl.BlockSpec((1, ATT_QKDIM), lambda j: (0, 0))
    return pl.pallas_call(
        functools.partial(_modulation_kernel, lam0=lam0),
        grid=(n // tn,),
        in_specs=[pl.BlockSpec((rows, d), lambda j: (0, 0)),
                  pl.BlockSpec((d, tn), lambda j: (0, j)),
                  pl.BlockSpec((1, tn), lambda j: (0, j)),
                  vec, vec, vec, vec],
        out_specs=[pl.BlockSpec((rows, tn), lambda j: (0, j)),
                   pl.BlockSpec((8, LANES), lambda j: (0, 0))],
        out_shape=[jax.ShapeDtypeStruct((rows, n), jnp.float32),
                   jax.ShapeDtypeStruct((8, LANES), jnp.float32)],
        compiler_params=pltpu.CompilerParams(dimension_semantics=("arbitrary",),
                                             vmem_limit_bytes=_VMEM_LIMIT),
        name="modulation",
    )(cond, w_ada, b_ada, lq1, lk1, lq2, lk2)


_QK_BLOCKS = ("rope_q", "rope_k", "key")


def _rope(acc, cos, sin):
    outs = []
    for p in range(acc.shape[1] // PAIR):
        x1 = acc[:, p * PAIR:p * PAIR + LANES]
        x2 = acc[:, p * PAIR + LANES:(p + 1) * PAIR]
        outs += [x1 * cos - x2 * sin, x2 * cos + x1 * sin]
    return jnp.concatenate(outs, axis=1)


def _in_proj_kernel(x_ref, shift_ref, scale_ref, g_ref, wqk_ref, wrest_ref, rope_ref, lng_ref, lnb_ref, o_ref,
                    h_ref, *, blocks, row_chunk):
    n = pl.program_id(1)
    tm = x_ref.shape[0]

    def epilogue(blk, acc, rows):
        if blk == "rope_q":
            return _rope(acc, rope_ref[0, rows, :], rope_ref[1, rows, :])
        if blk == "rope_k":
            return _rope(acc, rope_ref[2, rows, :], rope_ref[3, rows, :])
        if blk == "silu":
            return _silu(acc)
        if blk == "layernorm":
            mu = jnp.mean(acc, axis=-1, keepdims=True)
            xc = acc - mu
            var = jnp.mean(xc * xc, axis=-1, keepdims=True)
            return xc * lax.rsqrt(var + EPS) * lng_ref[...] + lnb_ref[...]
        return acc

    for idx, blk in enumerate(blocks):
        @pl.when(n == idx)
        def _(idx=idx, blk=blk):
            w_ref = wqk_ref if blk in _QK_BLOCKS else wrest_ref
            for r in range(tm // row_chunk):
                rows = slice(r * row_chunk, (r + 1) * row_chunk)
                if idx == 0:
                    xf = x_ref[rows, :]
                    ms = jnp.mean(xf * xf, axis=-1, keepdims=True)
                    y = xf * lax.rsqrt(ms + EPS) * g_ref[...]
                    hb = (y * (1.0 + scale_ref[0]) + shift_ref[0]).astype(jnp.bfloat16)
                    h_ref[rows, :] = hb
                else:
                    hb = h_ref[rows, :]
                acc = jnp.dot(hb, w_ref[...], preferred_element_type=jnp.float32)
                o_ref[rows, :] = epilogue(blk, acc, rows).astype(o_ref.dtype)


def _in_proj(x2d, mod3, mod_row, norm_g, w_qk, w_rest, rope, ln_g, ln_b, *, tm, row_chunk, blocks,
             qk_block, rest_block):
    m, d = x2d.shape
    nblk = len(blocks)
    tiles_per_seq = rope.shape[1] // tm
    return pl.pallas_call(
        functools.partial(_in_proj_kernel, blocks=blocks, row_chunk=row_chunk),
        grid=(m // tm, nblk),
        in_specs=[pl.BlockSpec((tm, d), lambda i, n: (i, 0)),
                  pl.BlockSpec((1, 1, d), lambda i, n: (mod_row(i), 0, 0)),
                  pl.BlockSpec((1, 1, d), lambda i, n: (mod_row(i), 0, 1)),
                  pl.BlockSpec((1, d), lambda i, n: (0, 0)),
                  pl.BlockSpec((d, COL_BLOCK), lambda i, n: (0, qk_block(n))),
                  pl.BlockSpec((d, COL_BLOCK), lambda i, n: (0, rest_block(n))),
                  pl.BlockSpec((4, tm, LANES), lambda i, n: (0, i % tiles_per_seq, 0)),
                  pl.BlockSpec((1, COL_BLOCK), lambda i, n: (0, 0)),
                  pl.BlockSpec((1, COL_BLOCK), lambda i, n: (0, 0))],
        out_specs=pl.BlockSpec((tm, COL_BLOCK), lambda i, n: (i, n)),
        out_shape=jax.ShapeDtypeStruct((m, nblk * COL_BLOCK), jnp.bfloat16),
        scratch_shapes=[pltpu.VMEM((tm, d), jnp.bfloat16)],
        compiler_params=pltpu.CompilerParams(dimension_semantics=("parallel", "arbitrary"),
                                             vmem_limit_bytes=_VMEM_LIMIT),
        name="in_proj",
    )(x2d, mod3, mod3, norm_g, w_qk, w_rest, rope, ln_g, ln_b)


def _attention_kernel(q_ref, k_ref, v_ref, kc_ref, vc_ref, ga_ref, sg_ref, lam_ref, o_ref,
                      kall_ref, vt_ref, s_ref, eb_ref, *, tq, key_chunk, sm_rows, post_scale):
    seq = k_ref.shape[1]
    keys = kall_ref.shape[0]
    n_blocks = q_ref.shape[1] // tq
    width = N_CHAINS * tq
    kall_ref[0:seq, :] = k_ref[0]
    kall_ref[seq:, :] = kc_ref[0]
    ones_rows = (lax.broadcasted_iota(jnp.int32, (VT_ROWS - ATT_VDIM, keys), 0) == 0).astype(jnp.bfloat16)
    for head in range(2):
        cols = slice(head * ATT_VDIM, (head + 1) * ATT_VDIM)
        base = head * VT_ROWS
        vt_ref[base:base + ATT_VDIM, 0:seq] = v_ref[0, :, cols].astype(jnp.float32).T.astype(jnp.bfloat16)
        vt_ref[base:base + ATT_VDIM, seq:] = vc_ref[0, :, cols].astype(jnp.float32).T.astype(jnp.bfloat16)
        vt_ref[base + ATT_VDIM:base + VT_ROWS, :] = ones_rows

    chain_of_row = (lax.broadcasted_iota(jnp.int32, (PAIR, 1), 0) % LANES) // MAP_LANES
    lam = lam_ref[0:1, 0:1]
    sg = sg_ref[...] * post_scale

    def block_rows(j):
        return slice(j * tq, (j + 1) * tq)

    def scores(j):
        q_t = q_ref[0, block_rows(j), :].astype(jnp.float32).T
        qz = jnp.concatenate([jnp.where(chain_of_row == c, q_t, 0.0) for c in range(N_CHAINS)],
                             axis=1).astype(jnp.bfloat16)
        s_slot = s_ref.at[j % 2]
        m8 = None
        for c in range(keys // key_chunk):
            kr = slice(c * key_chunk, (c + 1) * key_chunk)
            s_c = jnp.dot(kall_ref[kr, :], qz, preferred_element_type=jnp.float32)
            s_slot[kr, :] = s_c
            part = jnp.max(s_c.reshape(key_chunk // 8, 8, width), axis=0)
            m8 = part if m8 is None else jnp.maximum(m8, part)
        return jnp.max(m8, axis=0, keepdims=True)

    def exponentiate(j, mx):
        s_slot = s_ref.at[j % 2]
        e_slot = eb_ref.at[j % 2]
        for c in range(keys // sm_rows):
            kr = slice(c * sm_rows, (c + 1) * sm_rows)
            e_slot[kr, :] = jnp.exp2(s_slot[kr, :] - mx).astype(jnp.bfloat16)

    def finish(j):
        e_slot = eb_ref.at[j % 2]
        rows = block_rows(j)
        for head in range(2):
            hc = slice(2 * head * tq, 2 * (head + 1) * tq)
            pv = jnp.dot(vt_ref[head * VT_ROWS:(head + 1) * VT_ROWS, :], e_slot[:, hc],
                         preferred_element_type=jnp.float32)
            pv = pv[:ATT_VDIM, :] * (1.0 / pv[ATT_VDIM:ATT_VDIM + 1, :])
            a_t = pv[:, :tq] - lam * pv[:, tq:]
            ms = jnp.mean(a_t * a_t, axis=0, keepdims=True)
            a_t = a_t * lax.rsqrt(ms + EPS) * sg
            cols = slice(head * ATT_VDIM, (head + 1) * ATT_VDIM)
            o_ref[0, rows, cols] = (a_t.T * ga_ref[0, rows, cols].astype(jnp.float32)).astype(o_ref.dtype)

    mx = {}
    for t in range(n_blocks + 2):
        if t < n_blocks:
            mx[t] = scores(t)
        if 0 <= t - 1 < n_blocks:
            exponentiate(t - 1, mx.pop(t - 1))
        if 0 <= t - 2 < n_blocks:
            finish(t - 2)


def _attention(proj, ctxp, subln_col, lam_tile, *, tq, key_chunk, sm_rows, post_scale):
    b, l, _ = proj.shape
    nctx = ctxp.shape[1]
    pb = COL_BLOCK // PAIR
    return pl.pallas_call(
        functools.partial(_attention_kernel, tq=tq, key_chunk=key_chunk, sm_rows=sm_rows,
                          post_scale=post_scale),
        grid=(b, N_PAIRS),
        in_specs=[pl.BlockSpec((1, l, PAIR), lambda bi, p: (bi, 0, BLK_Q * pb + p)),
                  pl.BlockSpec((1, l, PAIR), lambda bi, p: (bi, 0, BLK_K * pb + p)),
                  pl.BlockSpec((1, l, PAIR), lambda bi, p: (bi, 0, BLK_V * pb + p)),
                  pl.BlockSpec((1, nctx, PAIR), lambda bi, p: (bi, 0, p)),
                  pl.BlockSpec((1, nctx, PAIR), lambda bi, p: (bi, 0, pb + p)),
                  pl.BlockSpec((1, l, PAIR), lambda bi, p: (bi, 0, BLK_GA * pb + p)),
                  pl.BlockSpec((ATT_VDIM, tq), lambda bi, p: (0, 0)),
                  pl.BlockSpec((8, LANES), lambda bi, p: (0, 0))],
        out_specs=pl.BlockSpec((1, l, PAIR), lambda bi, p: (bi, 0, p)),
        out_shape=jax.ShapeDtypeStruct((b, l, ATT_WIDTH), jnp.bfloat16),
        scratch_shapes=[pltpu.VMEM((l + nctx, PAIR), jnp.bfloat16),
                        pltpu.VMEM((2 * VT_ROWS, l + nctx), jnp.bfloat16),
                        pltpu.VMEM((S_SLOTS, l + nctx, N_CHAINS * tq), jnp.float32),
                        pltpu.VMEM((2, l + nctx, N_CHAINS * tq), jnp.bfloat16)],
        compiler_params=pltpu.CompilerParams(dimension_semantics=("parallel", "arbitrary"),
                                             vmem_limit_bytes=_VMEM_LIMIT),
        name="diff_attention",
    )(proj, proj, proj, ctxp, ctxp, proj, subln_col, lam_tile)


def _spatial_gate(u_ref, vn_ref, gb_ref, ws_ref, bs_ref, rows):
    cols = []
    for g in range(MLP_GROUPS):
        cs = slice(g * MLP_GDIM, (g + 1) * MLP_GDIM)
        z = jnp.dot(ws_ref[g], vn_ref[rows, cs], preferred_element_type=jnp.float32) + bs_ref[g]
        m = u_ref[rows, cs].astype(jnp.float32) * z * gb_ref[rows, cs].astype(jnp.float32)
        cols.append(m.astype(jnp.bfloat16))
    return jnp.concatenate(cols, axis=1)


def _out_proj_kernel(x_ref, ya_ref, u_ref, vn_ref, gb_ref, ws_ref, bs_ref, wa_ref, wb_ref, gate_ref, fg_ref,
                     o_ref, *, row_chunk):
    for r in range(x_ref.shape[0] // row_chunk):
        rows = slice(r * row_chunk, (r + 1) * row_chunk)
        yb = jnp.concatenate(
            [_spatial_gate(u_ref, vn_ref, gb_ref, ws_ref, bs_ref,
                           slice(r * row_chunk + c * CHUNK, r * row_chunk + (c + 1) * CHUNK))
             for c in range(row_chunk // CHUNK)], axis=0)
        y = (jnp.dot(ya_ref[rows, :], wa_ref[...], preferred_element_type=jnp.float32)
             + jnp.dot(yb, wb_ref[...], preferred_element_type=jnp.float32))
        res = x_ref[rows, :] + gate_ref[0] * y
        ms = jnp.mean(res * res, axis=-1, keepdims=True)
        o_ref[rows, :] = res * lax.rsqrt(ms + EPS) * fg_ref[...]


def _out_proj(x2d, ya, proj, ws_bf, bs_b, w_bf, mod3, final_g, *, tm, row_chunk, seq):
    m, d = x2d.shape
    tiles_per_seq = seq // tm
    assert tm % row_chunk == 0 and row_chunk % CHUNK == 0 and seq % tm == 0
    mlp_in = lambda col: pl.BlockSpec((tm, MLP_WIDTH), lambda i: (i, col))
    return pl.pallas_call(
        functools.partial(_out_proj_kernel, row_chunk=row_chunk),
        grid=(m // tm,),
        in_specs=[pl.BlockSpec((tm, d), lambda i: (i, 0)),
                  pl.BlockSpec((tm, ATT_WIDTH), lambda i: (i, 0)),
                  mlp_in(BLK_U), mlp_in(BLK_VB), mlp_in(BLK_GB),
                  pl.BlockSpec((MLP_GROUPS, CHUNK, CHUNK), lambda i: (0, 0, 0)),
                  pl.BlockSpec((MLP_GROUPS, CHUNK, LANES), lambda i: (0, 0, 0)),
                  pl.BlockSpec((ATT_WIDTH, d), lambda i: (0, 0)),
                  pl.BlockSpec((MLP_WIDTH, d), lambda i: (1, 0)),
                  pl.BlockSpec((1, 1, d), lambda i: (i // tiles_per_seq, 0, 2)),
                  pl.BlockSpec((1, d), lambda i: (0, 0))],
        out_specs=pl.BlockSpec((tm, d), lambda i: (i, 0)),
        out_shape=jax.ShapeDtypeStruct((m, d), jnp.float32),
        compiler_params=pltpu.CompilerParams(dimension_semantics=("parallel",),
                                             vmem_limit_bytes=_VMEM_LIMIT),
        name="out_proj",
    )(x2d, ya, proj, proj, proj, ws_bf, bs_b, w_bf, w_bf, mod3, final_g)


def _paired_layout(w_qk):
    d = w_qk.shape[0]
    n_freq = ATT_QKDIM // 4
    w = w_qk.reshape(d, 2, N_PAIRS, 2, 2, 2, 2, n_freq)
    return w.transpose(0, 1, 2, 6, 3, 4, 5, 7).reshape(d, 2 * ATT_WIDTH)


def _rope_tables(seq, q_scale):
    n_freq = ATT_QKDIM // 4
    inv = ROPE_THETA ** (-jnp.arange(n_freq, dtype=jnp.float32) / n_freq)
    t = jnp.arange(seq)
    row = (t // GRID_W).astype(jnp.float32)
    col = (t % GRID_W).astype(jnp.float32)
    ang = jnp.concatenate([row[:, None] * inv, col[:, None] * inv], axis=-1)
    ang = jnp.tile(ang, (1, LANES // MAP_LANES))
    cos, sin = jnp.cos(ang), jnp.sin(ang)
    return jnp.stack([cos * q_scale, sin * q_scale, cos, sin], axis=0)


def kernel(x, c, ctx, c_ctx, w_ada, b_ada, norm_g, w_in, w_out, lam_q1, lam_k1, lam_q2, lam_k2, subln_g,
           mlp_ln_g, mlp_ln_b, w_s, b_s, final_g):
    b, seq, d = x.shape
    nctx = ctx.shape[1]
    depth = w_ada.shape[0]
    assert depth == 1 and d == D_MODEL and seq % GRID_W == 0
    i = 0
    lam0 = 0.8 - 0.6 * math.exp(-0.3 * i)

    tm_in, chunk_in = 1024, 256
    tm_out, chunk_out = 512, 256
    tq = 256

    mod_rows = -(-(b + 1) // 8) * 8
    cond = jnp.concatenate([c, c_ctx[None, :], jnp.zeros((mod_rows - b - 1, d), c.dtype)], axis=0)
    mod, lam_tile = _modulation(cond, w_ada[i], b_ada[i][None, :], lam_q1[i][None, :], lam_k1[i][None, :],
                                lam_q2[i][None, :], lam_k2[i][None, :], lam0)
    mod3 = mod.reshape(mod_rows, 1, 3 * d)

    w_in_bf = w_in[i].astype(jnp.bfloat16)
    w_qk_bf = _paired_layout(w_in_bf[:, :2 * ATT_WIDTH])
    w_out_bf = w_out[i].astype(jnp.bfloat16)
    ws_bf = w_s[i].astype(jnp.bfloat16)
    bs_b = jnp.broadcast_to(b_s[i][:, :, None], (MLP_GROUPS, CHUNK, LANES))
    subln_col = jnp.broadcast_to(subln_g[i][:, None], (ATT_VDIM, tq))
    rope = _rope_tables(seq, ATT_QKDIM ** -0.5 * LOG2E)
    ng = norm_g[i][None, :]
    ln_g = mlp_ln_g[i][None, :]
    ln_b = mlp_ln_b[i][None, :]

    x2d = x.reshape(b * seq, d)
    proj = _in_proj(x2d, mod3, lambda t: t // (seq // tm_in), ng, w_qk_bf, w_in_bf, rope, ln_g, ln_b,
                    tm=tm_in, row_chunk=chunk_in,
                    blocks=("rope_q", "rope_k", "plain", "silu", "plain", "layernorm", "silu"),
                    qk_block=lambda n: jnp.minimum(n, 1), rest_block=lambda n: jnp.where(n < 2, BLK_GB, n))
    ctx2d = ctx.reshape(b * nctx, d)
    ctxp = _in_proj(ctx2d, mod3, lambda t: b, ng, w_qk_bf, w_in_bf, rope, ln_g, ln_b,
                    tm=tm_in, row_chunk=chunk_in, blocks=("key", "plain"),
                    qk_block=lambda n: 1, rest_block=lambda n: BLK_V)

    proj3 = proj.reshape(b, seq, D_IN)
    ctxp3 = ctxp.reshape(b, nctx, 2 * ATT_WIDTH)
    ya = _attention(proj3, ctxp3, subln_col, lam_tile, tq=tq, key_chunk=256, sm_rows=64,
                    post_scale=1.0 - lam0)
    out = _out_proj(x2d, ya.reshape(b * seq, ATT_WIDTH), proj, ws_bf, bs_b, w_out_bf, mod3,
                    final_g[None, :], tm=tm_out, row_chunk=chunk_out, seq=seq)
    return out.reshape(b, seq, d)
```

```python
import functools
import math

import jax
import jax.numpy as jnp
import numpy as np
from jax import lax
from jax.experimental import pallas as pl
from jax.experimental.pallas import tpu as pltpu

D_MODEL = 2048
GRID_W = 64
ATT_WIDTH = 1024
ATT_HEADS = 8
ATT_VDIM = 128
ATT_QKDIM = 64
MLP_WIDTH = 1024
MLP_GROUPS = 8
MLP_GDIM = 128
CHUNK = 128
ROPE_THETA = 10000.0
EPS = 1e-6
D_IN = 4 * ATT_WIDTH + 3 * MLP_WIDTH
LANES = 128
COL_BLOCK = 1024
PAIR = 2 * ATT_VDIM
N_PAIRS = ATT_HEADS // 2
MAP_LANES = ATT_QKDIM // 2
N_CHAINS = 4
S_SLOTS = 2
BF16_SUBLANES = 16
VT_ROWS = ATT_VDIM + BF16_SUBLANES
LOG2E = 1.4426950408889634

BLK_Q, BLK_K, BLK_V, BLK_GA, BLK_U, BLK_VB, BLK_GB = range(7)

_VMEM_LIMIT = 56 * 1024 * 1024


def _silu(v):
    return v * (1.0 / (1.0 + jnp.exp(-v)))


def _modulation_kernel(cond_ref, w_ref, b_ref, lq1_ref, lk1_ref, lq2_ref, lk2_ref, mod_ref, lam_ref, *, lam0):
    s = _silu(cond_ref[...]).astype(jnp.bfloat16)
    acc = jnp.dot(s, w_ref[...].astype(jnp.bfloat16), preferred_element_type=jnp.float32)
    mod_ref[...] = acc + b_ref[...]
    d1 = jnp.sum(lq1_ref[...] * lk1_ref[...], axis=-1, keepdims=True)
    d2 = jnp.sum(lq2_ref[...] * lk2_ref[...], axis=-1, keepdims=True)
    lam = jnp.exp(d1) - jnp.exp(d2) + lam0
    lam_ref[...] = jnp.broadcast_to(lam, lam_ref.shape)


def _modulation(cond, w_ada, b_ada, lq1, lk1, lq2, lk2, lam0):
    rows, d = cond.shape
    n = w_ada.shape[1]
    tn = 768
    vec = pl.BlockSpec((1, ATT_QKDIM), lambda j: (0, 0))
    return pl.pallas_call(
        functools.partial(_modulation_kernel, lam0=lam0),
        grid=(n // tn,),
        in_specs=[pl.BlockSpec((rows, d), lambda j: (0, 0)),
                  pl.BlockSpec((d, tn), lambda j: (0, j)),
                  pl.BlockSpec((1, tn), lambda j: (0, j)),
                  vec, vec, vec, vec],
        out_specs=[pl.BlockSpec((rows, tn), lambda j: (0, j)),
                   pl.BlockSpec((8, LANES), lambda j: (0, 0))],
        out_shape=[jax.ShapeDtypeStruct((rows, n), jnp.float32),
                   jax.ShapeDtypeStruct((8, LANES), jnp.float32)],
        compiler_params=pltpu.CompilerParams(dimension_semantics=("arbitrary",),
                                             vmem_limit_bytes=_VMEM_LIMIT),
        name="modulation",
    )(cond, w_ada, b_ada, lq1, lk1, lq2, lk2)


_QK_BLOCKS = ("rope_q", "rope_k", "key")


def _rope(acc, cos, sin):
    outs = []
    for p in range(acc.shape[1] // PAIR):
        x1 = acc[:, p * PAIR:p * PAIR + LANES]
        x2 = acc[:, p * PAIR + LANES:(p + 1) * PAIR]
        outs += [x1 * cos - x2 * sin, x2 * cos + x1 * sin]
    return jnp.concatenate(outs, axis=1)


def _in_proj_kernel(xlo_ref, xhi_ref, shift_ref, scale_ref, g_ref, wqk_ref, wrest_ref, rope_ref, lng_ref, lnb_ref,
                    o_ref, h_ref, *, blocks, row_chunks):
    n = pl.program_id(1)
    half = xlo_ref.shape[0]

    def epilogue(blk, acc, rows):
        if blk == "rope_q":
            return _rope(acc, rope_ref[0, rows, :], rope_ref[1, rows, :])
        if blk == "rope_k":
            return _rope(acc, rope_ref[2, rows, :], rope_ref[3, rows, :])
        if blk == "silu":
            return _silu(acc)
        if blk == "layernorm":
            mu = jnp.mean(acc, axis=-1, keepdims=True)
            xc = acc - mu
            var = jnp.mean(xc * xc, axis=-1, keepdims=True)
            return xc * lax.rsqrt(var + EPS) * lng_ref[...] + lnb_ref[...]
        return acc

    for idx, blk in enumerate(blocks):
        @pl.when(n == idx)
        def _(idx=idx, blk=blk):
            w_ref = wqk_ref if blk in _QK_BLOCKS else wrest_ref
            start = 0
            for size in row_chunks:
                rows = slice(start, start + size)
                if idx == 0:
                    xf = xlo_ref[rows, :] if start < half else xhi_ref[start - half:start - half + size, :]
                    ms = jnp.mean(xf * xf, axis=-1, keepdims=True)
                    y = xf * lax.rsqrt(ms + EPS) * g_ref[...]
                    hb = (y * (1.0 + scale_ref[0]) + shift_ref[0]).astype(jnp.bfloat16)
                    h_ref[rows, :] = hb
                else:
                    hb = h_ref[rows, :]
                acc = jnp.dot(hb, w_ref[...], preferred_element_type=jnp.float32)
                o_ref[rows, :] = epilogue(blk, acc, rows).astype(o_ref.dtype)
                start += size


def _in_proj(x2d, mod3, mod_row, norm_g, w_qk, w_rest, rope, ln_g, ln_b, *, tm, row_chunks, blocks,
             qk_block, rest_block):
    m, d = x2d.shape
    nblk = len(blocks)
    n_tiles = m // tm
    half = tm // 2
    tiles_per_seq = rope.shape[1] // tm
    assert sum(row_chunks) == tm and all(s % BF16_SUBLANES == 0 for s in row_chunks)
    assert half in np.cumsum(row_chunks), "a row chunk must not straddle the two x windows"

    def x_hi_block(i, n):
        return jnp.minimum(2 * i + 1 + 2 * (n == nblk - 1).astype(jnp.int32), 2 * n_tiles - 1)

    return pl.pallas_call(
        functools.partial(_in_proj_kernel, blocks=blocks, row_chunks=row_chunks),
        grid=(n_tiles, nblk),
        in_specs=[pl.BlockSpec((half, d), lambda i, n: (2 * i, 0)),
                  pl.BlockSpec((half, d), lambda i, n: (x_hi_block(i, n), 0)),
                  pl.BlockSpec((1, 1, d), lambda i, n: (mod_row(i), 0, 0)),
                  pl.BlockSpec((1, 1, d), lambda i, n: (mod_row(i), 0, 1)),
                  pl.BlockSpec((1, d), lambda i, n: (0, 0)),
                  pl.BlockSpec((d, COL_BLOCK), lambda i, n: (0, qk_block(n))),
                  pl.BlockSpec((d, COL_BLOCK), lambda i, n: (0, rest_block(n))),
                  pl.BlockSpec((4, tm, LANES), lambda i, n: (0, i % tiles_per_seq, 0)),
                  pl.BlockSpec((1, COL_BLOCK), lambda i, n: (0, 0)),
                  pl.BlockSpec((1, COL_BLOCK), lambda i, n: (0, 0))],
        out_specs=pl.BlockSpec((tm, COL_BLOCK), lambda i, n: (i, n)),
        out_shape=jax.ShapeDtypeStruct((m, nblk * COL_BLOCK), jnp.bfloat16),
        scratch_shapes=[pltpu.VMEM((tm, d), jnp.bfloat16)],
        compiler_params=pltpu.CompilerParams(dimension_semantics=("parallel", "arbitrary"),
                                             vmem_limit_bytes=_VMEM_LIMIT),
        name="in_proj",
    )(x2d, x2d, mod3, mod3, norm_g, w_qk, w_rest, rope, ln_g, ln_b)


def _attention_kernel(q_ref, k_ref, v_ref, kc_ref, vc_ref, ga_ref, sg_ref, lam_ref, o_ref,
                      kall_ref, vt_ref, s_ref, eb_ref, *, tq, key_chunk, sm_rows, post_scale):
    seq = k_ref.shape[1]
    keys = kall_ref.shape[0]
    n_blocks = q_ref.shape[1] // tq
    width = N_CHAINS * tq
    kall_ref[0:seq, :] = k_ref[0]
    kall_ref[seq:, :] = kc_ref[0]
    ones_rows = (lax.broadcasted_iota(jnp.int32, (VT_ROWS - ATT_VDIM, keys), 0) == 0).astype(jnp.bfloat16)
    for head in range(2):
        cols = slice(head * ATT_VDIM, (head + 1) * ATT_VDIM)
        base = head * VT_ROWS
        vt_ref[base:base + ATT_VDIM, 0:seq] = v_ref[0, :, cols].astype(jnp.float32).T.astype(jnp.bfloat16)
        vt_ref[base:base + ATT_VDIM, seq:] = vc_ref[0, :, cols].astype(jnp.float32).T.astype(jnp.bfloat16)
        vt_ref[base + ATT_VDIM:base + VT_ROWS, :] = ones_rows

    chain_of_row = (lax.broadcasted_iota(jnp.int32, (PAIR, 1), 0) % LANES) // MAP_LANES
    lam = lam_ref[0:1, 0:1]
    sg = sg_ref[...] * post_scale

    def block_rows(j):
        return slice(j * tq, (j + 1) * tq)

    def masked_q_t(j):
        q_t = q_ref[0, block_rows(j), :].astype(jnp.float32).T
        return jnp.concatenate([jnp.where(chain_of_row == c, q_t, 0.0) for c in range(N_CHAINS)],
                               axis=1).astype(jnp.bfloat16)

    def score_chunk(j, qz, c):
        kr = slice(c * key_chunk, (c + 1) * key_chunk)
        s_c = jnp.dot(kall_ref[kr, :], qz, preferred_element_type=jnp.float32)
        s_ref[j % 2, kr, :] = s_c
        return jnp.max(s_c.reshape(key_chunk // 8, 8, width), axis=0)

    def exp_chunk(j, mx, c):
        for r in range(key_chunk // sm_rows):
            kr = slice(c * key_chunk + r * sm_rows, c * key_chunk + (r + 1) * sm_rows)
            eb_ref[j % 2, kr, :] = jnp.exp2(s_ref[j % 2, kr, :] - mx).astype(jnp.bfloat16)

    def pv_chunk(j, head, c):
        kr = slice(c * key_chunk, (c + 1) * key_chunk)
        hc = slice(2 * head * tq, 2 * (head + 1) * tq)
        return jnp.dot(vt_ref[head * VT_ROWS:(head + 1) * VT_ROWS, kr], eb_ref[j % 2, kr, hc],
                       preferred_element_type=jnp.float32)

    def finish(j, head, pv):
        pv = pv[:ATT_VDIM, :] * (1.0 / pv[ATT_VDIM:ATT_VDIM + 1, :])
        a_t = pv[:, :tq] - lam * pv[:, tq:]
        ms = jnp.mean(a_t * a_t, axis=0, keepdims=True)
        a_t = a_t * lax.rsqrt(ms + EPS) * sg
        rows = block_rows(j)
        cols = slice(head * ATT_VDIM, (head + 1) * ATT_VDIM)
        o_ref[0, rows, cols] = (a_t.T * ga_ref[0, rows, cols].astype(jnp.float32)).astype(o_ref.dtype)

    mx = {}
    for t in range(n_blocks + 2):
        do_scores = t < n_blocks
        do_exp = 0 <= t - 1 < n_blocks
        do_pv = 0 <= t - 2 < n_blocks
        qz = masked_q_t(t) if do_scores else None
        m8 = None
        pv = [None, None]
        for c in range(keys // key_chunk):
            if do_scores:
                part = score_chunk(t, qz, c)
                m8 = part if m8 is None else jnp.maximum(m8, part)
            if do_exp:
                exp_chunk(t - 1, mx[t - 1], c)
            if do_pv:
                for head in range(2):
                    part = pv_chunk(t - 2, head, c)
                    pv[head] = part if pv[head] is None else part + pv[head]
        if do_scores:
            mx[t] = jnp.max(m8, axis=0, keepdims=True)
        if do_pv:
            for head in range(2):
                finish(t - 2, head, pv[head])


def _attention(proj, ctxp, subln_col, lam_tile, *, tq, key_chunk, sm_rows, post_scale):
    b, l, _ = proj.shape
    nctx = ctxp.shape[1]
    pb = COL_BLOCK // PAIR
    return pl.pallas_call(
        functools.partial(_attention_kernel, tq=tq, key_chunk=key_chunk, sm_rows=sm_rows,
                          post_scale=post_scale),
        grid=(b, N_PAIRS),
        in_specs=[pl.BlockSpec((1, l, PAIR), lambda bi, p: (bi, 0, BLK_Q * pb + p)),
                  pl.BlockSpec((1, l, PAIR), lambda bi, p: (bi, 0, BLK_K * pb + p)),
                  pl.BlockSpec((1, l, PAIR), lambda bi, p: (bi, 0, BLK_V * pb + p)),
                  pl.BlockSpec((1, nctx, PAIR), lambda bi, p: (bi, 0, p)),
                  pl.BlockSpec((1, nctx, PAIR), lambda bi, p: (bi, 0, pb + p)),
                  pl.BlockSpec((1, l, PAIR), lambda bi, p: (bi, 0, BLK_GA * pb + p)),
                  pl.BlockSpec((ATT_VDIM, tq), lambda bi, p: (0, 0)),
                  pl.BlockSpec((8, LANES), lambda bi, p: (0, 0))],
        out_specs=pl.BlockSpec((1, l, PAIR), lambda bi, p: (bi, 0, p)),
        out_shape=jax.ShapeDtypeStruct((b, l, ATT_WIDTH), jnp.bfloat16),
        scratch_shapes=[pltpu.VMEM((l + nctx, PAIR), jnp.bfloat16),
                        pltpu.VMEM((2 * VT_ROWS, l + nctx), jnp.bfloat16),
                        pltpu.VMEM((S_SLOTS, l + nctx, N_CHAINS * tq), jnp.float32),
                        pltpu.VMEM((2, l + nctx, N_CHAINS * tq), jnp.bfloat16)],
        compiler_params=pltpu.CompilerParams(dimension_semantics=("parallel", "arbitrary"),
                                             vmem_limit_bytes=_VMEM_LIMIT),
        name="diff_attention",
    )(proj, proj, proj, ctxp, ctxp, proj, subln_col, lam_tile)


def _spatial_gate(u_ref, vn_ref, gb_ref, ws_ref, bs_ref, rows):
    cols = []
    for g in range(MLP_GROUPS):
        cs = slice(g * MLP_GDIM, (g + 1) * MLP_GDIM)
        z = jnp.dot(ws_ref[g], vn_ref[rows, cs], preferred_element_type=jnp.float32) + bs_ref[g]
        m = u_ref[rows, cs].astype(jnp.float32) * z * gb_ref[rows, cs].astype(jnp.float32)
        cols.append(m.astype(jnp.bfloat16))
    return jnp.concatenate(cols, axis=1)


def _out_proj_kernel(x_ref, ya_ref, u_ref, vn_ref, gb_ref, ws_ref, bs_ref, wa_ref, wb_ref, gate_ref, fg_ref,
                     o_ref, *, row_chunk):
    for r in range(x_ref.shape[0] // row_chunk):
        rows = slice(r * row_chunk, (r + 1) * row_chunk)
        yb = jnp.concatenate(
            [_spatial_gate(u_ref, vn_ref, gb_ref, ws_ref, bs_ref,
                           slice(r * row_chunk + c * CHUNK, r * row_chunk + (c + 1) * CHUNK))
             for c in range(row_chunk // CHUNK)], axis=0)
        y = (jnp.dot(ya_ref[rows, :], wa_ref[...], preferred_element_type=jnp.float32)
             + jnp.dot(yb, wb_ref[...], preferred_element_type=jnp.float32))
        res = x_ref[rows, :] + gate_ref[0] * y
        ms = jnp.mean(res * res, axis=-1, keepdims=True)
        o_ref[rows, :] = res * lax.rsqrt(ms + EPS) * fg_ref[...]


def _out_proj(x2d, ya, proj, ws_bf, bs_b, w_bf, mod3, final_g, *, tm, row_chunk, seq):
    m, d = x2d.shape
    tiles_per_seq = seq // tm
    assert tm % row_chunk == 0 and row_chunk % CHUNK == 0 and seq % tm == 0
    mlp_in = lambda col: pl.BlockSpec((tm, MLP_WIDTH), lambda i: (i, col))
    return pl.pallas_call(
        functools.partial(_out_proj_kernel, row_chunk=row_chunk),
        grid=(m // tm,),
        in_specs=[pl.BlockSpec((tm, d), lambda i: (i, 0)),
                  pl.BlockSpec((tm, ATT_WIDTH), lambda i: (i, 0)),
                  mlp_in(BLK_U), mlp_in(BLK_VB), mlp_in(BLK_GB),
                  pl.BlockSpec((MLP_GROUPS, CHUNK, CHUNK), lambda i: (0, 0, 0)),
                  pl.BlockSpec((MLP_GROUPS, CHUNK, LANES), lambda i: (0, 0, 0)),
                  pl.BlockSpec((ATT_WIDTH, d), lambda i: (0, 0)),
                  pl.BlockSpec((MLP_WIDTH, d), lambda i: (1, 0)),
                  pl.BlockSpec((1, 1, d), lambda i: (i // tiles_per_seq, 0, 2)),
                  pl.BlockSpec((1, d), lambda i: (0, 0))],
        out_specs=pl.BlockSpec((tm, d), lambda i: (i, 0)),
        out_shape=jax.ShapeDtypeStruct((m, d), jnp.float32),
        compiler_params=pltpu.CompilerParams(dimension_semantics=("parallel",),
                                             vmem_limit_bytes=_VMEM_LIMIT),
        name="out_proj",
    )(x2d, ya, proj, proj, proj, ws_bf, bs_b, w_bf, w_bf, mod3, final_g)


def _paired_layout(w_qk):
    d = w_qk.shape[0]
    n_freq = ATT_QKDIM // 4
    w = w_qk.reshape(d, 2, N_PAIRS, 2, 2, 2, 2, n_freq)
    return w.transpose(0, 1, 2, 6, 3, 4, 5, 7).reshape(d, 2 * ATT_WIDTH)


def _rope_tables(seq, q_scale):
    n_freq = ATT_QKDIM // 4
    inv = ROPE_THETA ** (-jnp.arange(n_freq, dtype=jnp.float32) / n_freq)
    t = jnp.arange(seq)
    row = (t // GRID_W).astype(jnp.float32)
    col = (t % GRID_W).astype(jnp.float32)
    ang = jnp.concatenate([row[:, None] * inv, col[:, None] * inv], axis=-1)
    ang = jnp.tile(ang, (1, LANES // MAP_LANES))
    cos, sin = jnp.cos(ang), jnp.sin(ang)
    return jnp.stack([cos * q_scale, sin * q_scale, cos, sin], axis=0)


def kernel(x, c, ctx, c_ctx, w_ada, b_ada, norm_g, w_in, w_out, lam_q1, lam_k1, lam_q2, lam_k2, subln_g,
           mlp_ln_g, mlp_ln_b, w_s, b_s, final_g):
    b, seq, d = x.shape
    nctx = ctx.shape[1]
    depth = w_ada.shape[0]
    assert depth == 1 and d == D_MODEL and seq % GRID_W == 0
    i = 0
    lam0 = 0.8 - 0.6 * math.exp(-0.3 * i)

    tm_in = 1024
    chunks_in = (256, 256, 256, 256)
    tm_out, chunk_out = 512, 256
    tq = 256

    mod_rows = -(-(b + 1) // 8) * 8
    cond = jnp.concatenate([c, c_ctx[None, :], jnp.zeros((mod_rows - b - 1, d), c.dtype)], axis=0)
    mod, lam_tile = _modulation(cond, w_ada[i], b_ada[i][None, :], lam_q1[i][None, :], lam_k1[i][None, :],
                                lam_q2[i][None, :], lam_k2[i][None, :], lam0)
    mod3 = mod.reshape(mod_rows, 1, 3 * d)

    w_in_bf = w_in[i].astype(jnp.bfloat16)
    w_qk_bf = _paired_layout(w_in_bf[:, :2 * ATT_WIDTH])
    w_out_bf = w_out[i].astype(jnp.bfloat16)
    ws_bf = w_s[i].astype(jnp.bfloat16)
    bs_b = jnp.broadcast_to(b_s[i][:, :, None], (MLP_GROUPS, CHUNK, LANES))
    subln_col = jnp.broadcast_to(subln_g[i][:, None], (ATT_VDIM, tq))
    rope = _rope_tables(seq, ATT_QKDIM ** -0.5 * LOG2E)
    ng = norm_g[i][None, :]
    ln_g = mlp_ln_g[i][None, :]
    ln_b = mlp_ln_b[i][None, :]

    x2d = x.reshape(b * seq, d)
    proj = _in_proj(x2d, mod3, lambda t: t // (seq // tm_in), ng, w_qk_bf, w_in_bf, rope, ln_g, ln_b,
                    tm=tm_in, row_chunks=chunks_in,
                    blocks=("rope_q", "rope_k", "plain", "silu", "plain", "layernorm", "silu"),
                    qk_block=lambda n: jnp.minimum(n, 1), rest_block=lambda n: jnp.where(n < 2, BLK_GB, n))
    ctx2d = ctx.reshape(b * nctx, d)
    ctxp = _in_proj(ctx2d, mod3, lambda t: b, ng, w_qk_bf, w_in_bf, rope, ln_g, ln_b,
                    tm=tm_in, row_chunks=chunks_in, blocks=("key", "plain"),
                    qk_block=lambda n: 1, rest_block=lambda n: BLK_V)

    proj3 = proj.reshape(b, seq, D_IN)
    ctxp3 = ctxp.reshape(b, nctx, 2 * ATT_WIDTH)
    ya = _attention(proj3, ctxp3, subln_col, lam_tile, tq=tq, key_chunk=768, sm_rows=64,
                    post_scale=1.0 - lam0)
    out = _out_proj(x2d, ya.reshape(b * seq, ATT_WIDTH), proj, ws_bf, bs_b, w_out_bf, mod3,
                    final_g[None, :], tm=tm_out, row_chunk=chunk_out, seq=seq)
    return out.reshape(b, seq, d)
```
